```python
import jax, jax.numpy as jnp
from jax import lax
import numpy as np

D_MODEL = 1024
BATCH = 8
SEQ = 4096
DEPTH = 4

N_MIXERS = 3
EPS = 1e-6
NEG = -1e30

ATT_HEADS = 8
ATT_HEAD_DIM = D_MODEL // ATT_HEADS
ATT_WIDTH = ATT_HEADS * ATT_HEAD_DIM
MOBA_BLOCK = 256
MOBA_TOPK = 3
MOBA_QCHUNK = 16

CONV_CHANNELS = D_MODEL
CONV_KERNEL = 31

LRU_WIDTH = 1280
LRU_HEADS = 10
LRU_HEAD_DIM = LRU_WIDTH // LRU_HEADS
LRU_CONV = 4
LRU_C = 8.0

kernel_name = 'hybrid_moba_conformer_rglru_trunk'


def rmsnorm(x, g):
    xf = x.astype(jnp.float32)
    y = xf * lax.rsqrt(jnp.mean(xf * xf, axis=-1, keepdims=True) + EPS)
    return (y * g).astype(x.dtype)


def layernorm(x, g, b):
    xf = x.astype(jnp.float32)
    mu = jnp.mean(xf, axis=-1, keepdims=True)
    var = jnp.mean(jnp.square(xf - mu), axis=-1, keepdims=True)
    return ((xf - mu) * lax.rsqrt(var + EPS) * g + b).astype(x.dtype)


def causal_depthwise_conv(x, w, b):
    width = w.shape[0]
    y = lax.conv_general_dilated(
        x, w[:, None, :].astype(x.dtype), window_strides=(1,), padding=[(width - 1, 0)],
        dimension_numbers=('NWC', 'WIO', 'NWC'), feature_group_count=x.shape[-1])
    return y + b


def moba_mixer(u, w_in, w_out):
    bsz, seq, _ = u.shape
    q, k, v, gate = jnp.split(u @ w_in, 4, axis=-1)
    nb = -(-seq // MOBA_BLOCK)
    s_pad = nb * MOBA_BLOCK

    def heads(t):
        t = jnp.pad(t, ((0, 0), (0, s_pad - seq), (0, 0)))
        return t.reshape(bsz, s_pad, ATT_HEADS, ATT_HEAD_DIM).transpose(0, 2, 1, 3)

    q = heads(q) * (ATT_HEAD_DIM ** -0.5)
    k, v = heads(k), heads(v)
    kb = k.reshape(bsz, ATT_HEADS, nb, MOBA_BLOCK, ATT_HEAD_DIM)
    vb = v.reshape(bsz, ATT_HEADS, nb, MOBA_BLOCK, ATT_HEAD_DIM)

    k_mean = jnp.mean(kb.astype(jnp.float32), axis=3)
    scores = jnp.einsum('bhsd,bhnd->bhsn', q.astype(jnp.float32), k_mean)
    q_blk = jnp.arange(s_pad) // MOBA_BLOCK
    past = jnp.arange(nb)[None, :] < q_blk[:, None]
    scores = jnp.where(past, scores, NEG)
    n_sel = min(MOBA_TOPK, nb)
    _, sel = lax.top_k(scores, n_sel)

    n_chunks = s_pad // MOBA_QCHUNK
    q_ch = q.reshape(bsz, ATT_HEADS, n_chunks, MOBA_QCHUNK, ATT_HEAD_DIM).transpose(2, 0, 1, 3, 4)
    sel_ch = sel.reshape(bsz, ATT_HEADS, n_chunks, MOBA_QCHUNK, n_sel).transpose(2, 0, 1, 3, 4)
    gather_blocks = jax.vmap(jax.vmap(lambda blocks, idx: blocks[idx]))

    def attend_chunk(args):
        c, q_c, sel_c = args
        start = c * MOBA_QCHUNK
        blk = start // MOBA_BLOCK
        pos_q = start + jnp.arange(MOBA_QCHUNK)
        k_sel = gather_blocks(kb, sel_c)
        v_sel = gather_blocks(vb, sel_c)
        s_sel = jnp.einsum('bhqd,bhqnkd->bhqnk', q_c, k_sel).astype(jnp.float32)
        valid = (jnp.arange(n_sel) < blk)[:, None]
        s_sel = jnp.where(valid, s_sel, NEG).reshape(bsz, ATT_HEADS, MOBA_QCHUNK, n_sel * MOBA_BLOCK)
        k_own = lax.dynamic_index_in_dim(kb, blk, axis=2, keepdims=False)
        v_own = lax.dynamic_index_in_dim(vb, blk, axis=2, keepdims=False)
        s_own = jnp.einsum('bhqd,bhkd->bhqk', q_c, k_own).astype(jnp.float32)
        pos_k = blk * MOBA_BLOCK + jnp.arange(MOBA_BLOCK)
        s_own = jnp.where(pos_k[None, :] <= pos_q[:, None], s_own, NEG)
        p = jax.nn.softmax(jnp.concatenate([s_sel, s_own], axis=-1), axis=-1)
        p_sel = p[..., :n_sel * MOBA_BLOCK].reshape(bsz, ATT_HEADS, MOBA_QCHUNK, n_sel, MOBA_BLOCK)
        p_own = p[..., n_sel * MOBA_BLOCK:]
        o = (jnp.einsum('bhqnk,bhqnkd->bhqd', p_sel.astype(v.dtype), v_sel)
             + jnp.einsum('bhqk,bhkd->bhqd', p_own.astype(v.dtype), v_own))
        return o

    o = lax.map(attend_chunk, (jnp.arange(n_chunks), q_ch, sel_ch))
    o = o.transpose(1, 0, 3, 2, 4).reshape(bsz, s_pad, ATT_WIDTH)[:, :seq]
    return (o * jax.nn.silu(gate)) @ w_out


def conformer_conv_mixer(u, w_in, conv_w, conv_b, ln_g, ln_b, w_out):
    a, b, gate = jnp.split(u @ w_in, 3, axis=-1)
    y = a * jax.nn.sigmoid(b)
    y = causal_depthwise_conv(y, conv_w, conv_b)
    y = jax.nn.silu(layernorm(y, ln_g, ln_b))
    return (y * jax.nn.silu(gate)) @ w_out


def _linear_recurrence(c1, c2):
    a1, b1 = c1
    a2, b2 = c2
    return a1 * a2, a2 * b1 + b2


def rglru_mixer(u, w_in, conv_w, conv_b, w_rg, b_rg, w_ig, b_ig, lam, w_out):
    bsz, seq, _ = u.shape
    xb, gate = jnp.split(u @ w_in, 2, axis=-1)
    xb = causal_depthwise_conv(xb, conv_w, conv_b)
    xf = xb.astype(jnp.float32)
    xh = xf.reshape(bsz, seq, LRU_HEADS, LRU_HEAD_DIM)
    r = jax.nn.sigmoid(jnp.einsum('bshi,hij->bshj', xh, w_rg.astype(jnp.float32)).reshape(bsz, seq, LRU_WIDTH) + b_rg)
    i = jax.nn.sigmoid(jnp.einsum('bshi,hij->bshj', xh, w_ig.astype(jnp.float32)).reshape(bsz, seq, LRU_WIDTH) + b_ig)
    log_a = -LRU_C * r * jax.nn.softplus(-lam.astype(jnp.float32))
    a = jnp.exp(log_a)
    bterm = jnp.sqrt(-jnp.expm1(2.0 * log_a)) * (i * xf)
    _, h = lax.associative_scan(_linear_recurrence, (a, bterm), axis=1)
    y = h.astype(u.dtype) * jax.nn.silu(gate)
    return y @ w_out


def _normal(key, shape, scale):
    return jax.random.normal(key, shape, jnp.float32) * scale


def _attn_params(key, p):
    k = jax.random.split(key, 3)
    return {
        p + 'norm_g': 1.0 + _normal(k[0], (D_MODEL,), 0.02),
        p + 'w_in': _normal(k[1], (D_MODEL, 4 * ATT_WIDTH), D_MODEL ** -0.5),
        p + 'w_out': _normal(k[2], (ATT_WIDTH, D_MODEL), ATT_WIDTH ** -0.5),
    }


def _conv_params(key, p):
    k = jax.random.split(key, 7)
    return {
        p + 'norm_g': 1.0 + _normal(k[0], (D_MODEL,), 0.02),
        p + 'w_in': _normal(k[1], (D_MODEL, 3 * CONV_CHANNELS), D_MODEL ** -0.5),
        p + 'conv_w': _normal(k[2], (CONV_KERNEL, CONV_CHANNELS), CONV_KERNEL ** -0.5),
        p + 'conv_b': _normal(k[3], (CONV_CHANNELS,), 0.02),
        p + 'ln_g': 1.0 + _normal(k[4], (CONV_CHANNELS,), 0.02),
        p + 'ln_b': _normal(k[5], (CONV_CHANNELS,), 0.02),
        p + 'w_out': _normal(k[6], (CONV_CHANNELS, D_MODEL), CONV_CHANNELS ** -0.5),
    }


def _lru_params(key, p):
    k = jax.random.split(key, 10)
    a_c = jax.random.uniform(k[8], (LRU_WIDTH,), jnp.float32, 0.9, 0.999)
    a_base = a_c ** (1.0 / LRU_C)
    return {
        p + 'norm_g': 1.0 + _normal(k[0], (D_MODEL,), 0.02),
        p + 'w_in': _normal(k[1], (D_MODEL, 2 * LRU_WIDTH), D_MODEL ** -0.5),
        p + 'conv_w': _normal(k[2], (LRU_CONV, LRU_WIDTH), LRU_CONV ** -0.5),
        p + 'conv_b': _normal(k[3], (LRU_WIDTH,), 0.02),
        p + 'w_rg': _normal(k[4], (LRU_HEADS, LRU_HEAD_DIM, LRU_HEAD_DIM), LRU_HEAD_DIM ** -0.5),
        p + 'b_rg': _normal(k[5], (LRU_WIDTH,), 0.1),
        p + 'w_ig': _normal(k[6], (LRU_HEADS, LRU_HEAD_DIM, LRU_HEAD_DIM), LRU_HEAD_DIM ** -0.5),
        p + 'b_ig': _normal(k[7], (LRU_WIDTH,), 0.1),
        p + 'lam': jnp.log(a_base) - jnp.log1p(-a_base),
        p + 'w_out': _normal(k[9], (LRU_WIDTH, D_MODEL), LRU_WIDTH ** -0.5),
    }


def setup_inputs(seed: int = 0) -> dict:
    key = jax.random.key(seed)
    keys = jax.random.split(key, DEPTH + 2)
    builders = (_attn_params, _conv_params, _lru_params)
    params = {'x': jax.random.normal(keys[0], (BATCH, SEQ, D_MODEL), jnp.float32)}
    for i in range(DEPTH):
        params.update(builders[i % N_MIXERS](keys[i + 1], 'l%d_' % i))
    params['final_g'] = 1.0 + _normal(keys[DEPTH + 1], (D_MODEL,), 0.02)
    return params


def reference(x, l0_norm_g, l0_w_in, l0_w_out,
              l1_norm_g, l1_w_in, l1_conv_w, l1_conv_b, l1_ln_g, l1_ln_b, l1_w_out,
              l2_norm_g, l2_w_in, l2_conv_w, l2_conv_b, l2_w_rg, l2_b_rg, l2_w_ig, l2_b_ig, l2_lam, l2_w_out,
              l3_norm_g, l3_w_in, l3_w_out,
              final_g):
    layers = (
        (l0_norm_g, (l0_w_in, l0_w_out)),
        (l1_norm_g, (l1_w_in, l1_conv_w, l1_conv_b, l1_ln_g, l1_ln_b, l1_w_out)),
        (l2_norm_g, (l2_w_in, l2_conv_w, l2_conv_b, l2_w_rg, l2_b_rg, l2_w_ig, l2_b_ig, l2_lam, l2_w_out)),
        (l3_norm_g, (l3_w_in, l3_w_out)),
    )
    mixers = (moba_mixer, conformer_conv_mixer, rglru_mixer)
    h = x
    for i in range(DEPTH):
        g, p = layers[i]
        h = h + mixers[i % N_MIXERS](rmsnorm(h, g), *p)
    return rmsnorm(h, final_g)
```

```python
import functools

import jax
import jax.numpy as jnp
from jax import lax
from jax.experimental import pallas as pl
from jax.experimental.pallas import tpu as pltpu

F32 = jnp.float32
BF16 = jnp.bfloat16

EPS = 1e-6
NEG = -1e30

ATT_HEADS = 8
MOBA_BLOCK = 256
MOBA_TOPK = 3
LRU_HEADS = 10
LRU_C = 8.0

VMEM_LIMIT_BYTES = 56 * 1024 * 1024


def _const_spec(shape):
    zeros = (0,) * len(shape)
    return pl.BlockSpec(shape, lambda *_: zeros, pipeline_mode=pl.Buffered(1))


def _params(n_axes):
    return pltpu.CompilerParams(
        dimension_semantics=("arbitrary",) * n_axes,
        vmem_limit_bytes=VMEM_LIMIT_BYTES,
    )


def _rmsnorm(x, g):
    return x * lax.rsqrt(jnp.mean(x * x, axis=-1, keepdims=True) + EPS) * g


def _sigmoid(x):
    return jax.nn.sigmoid(x)


def _silu(x):
    return x * jax.nn.sigmoid(x)


def _rms_qkvg_kernel(x_ref, g_ref, w_ref, q_ref, k_ref, v_ref, gate_ref, *, q_scale):
    xn = _rmsnorm(x_ref[...], g_ref[...]).astype(BF16)
    width = q_ref.shape[1]
    for c, ref in enumerate((q_ref, k_ref, v_ref, gate_ref)):
        acc = jnp.dot(xn, w_ref[:, c * width:(c + 1) * width], preferred_element_type=F32)
        if c == 0:
            acc = acc * q_scale
        ref[...] = acc.astype(ref.dtype)


def _rms_qkvg(h, g, w_in, *, head_dim, tm=512):
    n, d = h.shape
    width = w_in.shape[1] // 4
    out_bf = jax.ShapeDtypeStruct((n, width), BF16)
    row_spec = pl.BlockSpec((tm, width), lambda i: (i, 0))
    return pl.pallas_call(
        functools.partial(_rms_qkvg_kernel, q_scale=head_dim ** -0.5),
        grid=(n // tm,),
        in_specs=[
            pl.BlockSpec((tm, d), lambda i: (i, 0)),
            _const_spec((1, d)),
            _const_spec(w_in.shape),
        ],
        out_specs=[row_spec, row_spec, row_spec, row_spec],
        out_shape=[out_bf, out_bf, out_bf, jax.ShapeDtypeStruct((n, width), F32)],
        compiler_params=_params(1),
        name="rms_qkvg",
    )(h, g, w_in)


_TRANS_B = (((1,), (1,)), ((), ()))
_TRANS_A = (((0,), (0,)), ((), ()))


def _moba_attn_kernel(q_ref, k_ref, v_ref, gate_ref, o_ref, kmean_ref, sel_ref, *, blk, nb, topk):
    i = pl.program_id(2)

    @pl.when(i == 0)
    def _():
        for j in range(nb):
            kb = k_ref[j * blk:(j + 1) * blk, :].astype(F32)
            kmean_ref[j:j + 1, :] = jnp.mean(kb, axis=0, keepdims=True)

    q = q_ref[...]

    km = kmean_ref[...]
    km_hi = km.astype(BF16)
    km_lo = (km - km_hi.astype(F32)).astype(BF16)
    sc = (lax.dot_general(km_hi, q, _TRANS_B, preferred_element_type=F32)
          + lax.dot_general(km_lo, q, _TRANS_B, preferred_element_type=F32))
    n_iota = lax.broadcasted_iota(jnp.int32, (nb, blk), 0)
    past = n_iota < i
    sc = jnp.where(past, sc, NEG)
    rank = jnp.zeros((nb, blk), jnp.int32)
    for m in range(nb):
        row = sc[m:m + 1, :]
        beats = (row > sc) | ((row == sc) & (m < n_iota))
        rank = rank + beats.astype(jnp.int32)
    sel_ref[...] = ((rank < topk) & past).astype(F32)

    def scores(j):
        kj = k_ref[pl.ds(pl.multiple_of(j * blk, blk), blk), :]
        return lax.dot_general(kj, q, _TRANS_B, preferred_element_type=F32)

    def values(j):
        return v_ref[pl.ds(pl.multiple_of(j * blk, blk), blk), :]

    s = scores(i)
    k_idx = lax.broadcasted_iota(jnp.int32, (blk, blk), 0)
    q_idx = lax.broadcasted_iota(jnp.int32, (blk, blk), 1)
    s = jnp.where(k_idx <= q_idx, s, NEG)
    m0 = jnp.max(s, axis=0, keepdims=True)
    p = jnp.exp(s - m0)
    l0 = jnp.sum(p, axis=0, keepdims=True)
    acc0 = lax.dot_general(values(i), p.astype(BF16), _TRANS_A, preferred_element_type=F32)

    def body(j, carry):
        m_run, l_run, acc = carry
        s = scores(j)
        s = jnp.where(sel_ref[pl.ds(j, 1), :] > 0.5, s, NEG)
        m_new = jnp.maximum(m_run, jnp.max(s, axis=0, keepdims=True))
        alpha = jnp.exp(m_run - m_new)
        p = jnp.exp(s - m_new)
        l_new = alpha * l_run + jnp.sum(p, axis=0, keepdims=True)
        pv = lax.dot_general(values(j), p.astype(BF16), _TRANS_A, preferred_element_type=F32)
        return m_new, l_new, alpha * acc + pv

    _, l_fin, acc = lax.fori_loop(0, i, body, (m0, l0, acc0))
    o = (acc / l_fin).T
    o_ref[...] = (o * _silu(gate_ref[...])).astype(o_ref.dtype)


def _moba_attention(q, k, v, gate, *, batch, seq):
    n, width = q.shape
    dh = width // ATT_HEADS
    blk = MOBA_BLOCK
    assert seq % blk == 0
    nb = seq // blk
    qspec = pl.BlockSpec((blk, dh), lambda b, h, i: (b * nb + i, h))
    kvspec = pl.BlockSpec((seq, dh), lambda b, h, i: (b, h))
    return pl.pallas_call(
        functools.partial(_moba_attn_kernel, blk=blk, nb=nb, topk=MOBA_TOPK),
        grid=(batch, ATT_HEADS, nb),
        in_specs=[qspec, kvspec, kvspec, qspec],
        out_specs=qspec,
        out_shape=jax.ShapeDtypeStruct((n, width), BF16),
        scratch_shapes=[pltpu.VMEM((nb, dh), F32), pltpu.VMEM((nb, blk), F32)],
        compiler_params=_params(3),
        name="moba_attention",
    )(q, k, v, gate)


def _out_proj_kernel(h_ref, a_ref, w_ref, *rest):
    o_ref = rest[-1]
    y = h_ref[...] + jnp.dot(a_ref[...], w_ref[...], preferred_element_type=F32)
    if len(rest) == 2:
        y = _rmsnorm(y, rest[0][...])
    o_ref[...] = y


def _out_proj_residual(h, a, w_out, final_g=None, *, tm=512):
    n, d = h.shape
    kdim = a.shape[1]
    in_specs = [
        pl.BlockSpec((tm, d), lambda i: (i, 0)),
        pl.BlockSpec((tm, kdim), lambda i: (i, 0)),
        _const_spec(w_out.shape),
    ]
    args = [h, a, w_out]
    if final_g is not None:
        in_specs.append(_const_spec((1, d)))
        args.append(final_g)
    return pl.pallas_call(
        _out_proj_kernel,
        grid=(n // tm,),
        in_specs=in_specs,
        out_specs=pl.BlockSpec((tm, d), lambda i: (i, 0)),
        out_shape=jax.ShapeDtypeStruct((n, d), F32),
        compiler_params=_params(1),
        name="out_proj_residual",
    )(*args)


_CONV_HIST = 32


def _conformer_kernel(h_ref, g_ref, w_in_ref, cw_ref, cb_ref, lng_ref, lnb_ref, w_out_ref, o_ref,
                      ybuf_ref, *, tiles_per_seq, kw):
    tm, c = o_ref.shape
    hist = _CONV_HIST

    @pl.when(pl.program_id(0) % tiles_per_seq == 0)
    def _():
        ybuf_ref[0:hist, :] = jnp.zeros((hist, c), F32)

    h = h_ref[...]
    xn = _rmsnorm(h, g_ref[...]).astype(BF16)
    a = jnp.dot(xn, w_in_ref[:, 0:c], preferred_element_type=F32)
    b = jnp.dot(xn, w_in_ref[:, c:2 * c], preferred_element_type=F32)
    ybuf_ref[hist:hist + tm, :] = a * _sigmoid(b)

    base = hist - (kw - 1)
    acc = cb_ref[...] + cw_ref[0:1, :] * ybuf_ref[base:base + tm, :]
    for j in range(1, kw):
        acc = acc + cw_ref[j:j + 1, :] * ybuf_ref[base + j:base + j + tm, :]
    ybuf_ref[0:hist, :] = ybuf_ref[tm:tm + hist, :]

    mu = jnp.mean(acc, axis=-1, keepdims=True)
    cen = acc - mu
    var = jnp.mean(cen * cen, axis=-1, keepdims=True)
    y = cen * lax.rsqrt(var + EPS) * lng_ref[...] + lnb_ref[...]
    gate = jnp.dot(xn, w_in_ref[:, 2 * c:3 * c], preferred_element_type=F32)
    z = (_silu(y) * _silu(gate)).astype(BF16)
    o_ref[...] = h + jnp.dot(z, w_out_ref[...], preferred_element_type=F32)


def _conformer_layer(h, g, w_in, conv_w, conv_b, ln_g, ln_b, w_out, *, seq, tm=256):
    n, d = h.shape
    c = w_out.shape[0]
    kw = conv_w.shape[0]
    assert kw - 1 <= _CONV_HIST <= tm and seq % tm == 0
    return pl.pallas_call(
        functools.partial(_conformer_kernel, tiles_per_seq=seq // tm, kw=kw),
        grid=(n // tm,),
        in_specs=[
            pl.BlockSpec((tm, d), lambda i: (i, 0)),
            _const_spec((1, d)),
            _const_spec(w_in.shape),
            _const_spec(conv_w.shape),
            _const_spec((1, c)),
            _const_spec((1, c)),
            _const_spec((1, c)),
            _const_spec(w_out.shape),
        ],
        out_specs=pl.BlockSpec((tm, d), lambda i: (i, 0)),
        out_shape=jax.ShapeDtypeStruct((n, d), F32),
        scratch_shapes=[pltpu.VMEM((_CONV_HIST + tm, c), F32)],
        compiler_params=_params(1),
        name="conformer_layer",
    )(h, g, w_in, conv_w, conv_b, ln_g, ln_b, w_out)


_LRU_HIST = 8
_SUBLANES = 8


def _rglru_kernel(h_ref, g_ref, w_in_ref, cw_ref, cb_ref, w_gates_ref, b_rg_ref, b_ig_ref, lam_ref,
                  w_out_ref, o_ref, xbuf_ref, a_ref, b_ref, hs_ref, state_ref, *, tiles_per_seq, kw, heads):
    tm = o_ref.shape[0]
    w = w_out_ref.shape[0]
    hd = w // heads
    hist = _LRU_HIST

    @pl.when(pl.program_id(0) % tiles_per_seq == 0)
    def _():
        xbuf_ref[0:hist, :] = jnp.zeros((hist, w), F32)
        state_ref[...] = jnp.zeros_like(state_ref)

    h = h_ref[...]
    xn = _rmsnorm(h, g_ref[...]).astype(BF16)
    xbuf_ref[hist:hist + tm, :] = jnp.dot(xn, w_in_ref[:, 0:w], preferred_element_type=F32)

    base = hist - (kw - 1)
    xc = cb_ref[...] + cw_ref[0:1, :] * xbuf_ref[base:base + tm, :]
    for j in range(1, kw):
        xc = xc + cw_ref[j:j + 1, :] * xbuf_ref[base + j:base + j + tm, :]
    xbuf_ref[0:hist, :] = xbuf_ref[tm:tm + hist, :]

    neg_lam = -lam_ref[...]
    c_softplus = LRU_C * (jnp.maximum(neg_lam, 0.0) + jnp.log1p(jnp.exp(-jnp.abs(neg_lam))))
    xc_bf = xc.astype(BF16)
    for hh in range(heads):
        cols = slice(hh * hd, (hh + 1) * hd)
        gates = jnp.dot(xc_bf[:, cols], w_gates_ref[hh], preferred_element_type=F32)
        r = _sigmoid(gates[:, 0:hd] + b_rg_ref[:, cols])
        ig = _sigmoid(gates[:, hd:2 * hd] + b_ig_ref[:, cols])
        log_a = -(r * c_softplus[:, cols])
        a = jnp.exp(log_a)
        a_ref[:, cols] = a
        one_minus_a2 = -jnp.tanh(log_a) * (a * a + 1.0)
        b_ref[:, cols] = jnp.sqrt(one_minus_a2) * (ig * xc[:, cols])

    row = lax.broadcasted_iota(jnp.int32, (_SUBLANES, w), 0)

    def scan_group(r, carry):
        r0 = pl.multiple_of(r * _SUBLANES, _SUBLANES)
        av = a_ref[pl.ds(r0, _SUBLANES), :]
        bv = b_ref[pl.ds(r0, _SUBLANES), :]
        for s in (1, 2, 4):
            a_prev = jnp.where(row >= s, pltpu.roll(av, s, 0), 1.0)
            b_prev = jnp.where(row >= s, pltpu.roll(bv, s, 0), 0.0)
            bv = av * b_prev + bv
            av = av * a_prev
        hv = av * carry + bv
        hs_ref[pl.ds(r0, _SUBLANES), :] = hv
        return hv[_SUBLANES - 1:_SUBLANES, :]

    state_ref[...] = lax.fori_loop(0, tm // _SUBLANES, scan_group, state_ref[...])

    gate = jnp.dot(xn, w_in_ref[:, w:2 * w], preferred_element_type=F32)
    y = (hs_ref[...] * _silu(gate)).astype(BF16)
    o_ref[...] = h + jnp.dot(y, w_out_ref[...], preferred_element_type=F32)


def _rglru_layer(h, g, w_in, conv_w, conv_b, w_gates, b_rg, b_ig, lam, w_out, *, seq, tm=256):
    n, d = h.shape
    w = w_out.shape[0]
    kw = conv_w.shape[0]
    assert kw - 1 <= _LRU_HIST <= tm and seq % tm == 0 and tm % _SUBLANES == 0
    return pl.pallas_call(
        functools.partial(_rglru_kernel, tiles_per_seq=seq // tm, kw=kw, heads=LRU_HEADS),
        grid=(n // tm,),
        in_specs=[
            pl.BlockSpec((tm, d), lambda i: (i, 0)),
            _const_spec((1, d)),
            _const_spec(w_in.shape),
            _const_spec(conv_w.shape),
            _const_spec((1, w)),
            _const_spec(w_gates.shape),
            _const_spec((1, w)),
            _const_spec((1, w)),
            _const_spec((1, w)),
            _const_spec(w_out.shape),
        ],
        out_specs=pl.BlockSpec((tm, d), lambda i: (i, 0)),
        out_shape=jax.ShapeDtypeStruct((n, d), F32),
        scratch_shapes=[
            pltpu.VMEM((_LRU_HIST + tm, w), F32),
            pltpu.VMEM((tm, w), F32),
            pltpu.VMEM((tm, w), F32),
            pltpu.VMEM((tm, w), F32),
            pltpu.VMEM((1, w), F32),
        ],
        compiler_params=_params(1),
        name="rglru_layer",
    )(h, g, w_in, conv_w, conv_b, w_gates, b_rg, b_ig, lam, w_out)


def _moba_layer(h, norm_g, w_in, w_out, *, batch, seq, final_g=None):
    head_dim = w_out.shape[0] // ATT_HEADS
    q, k, v, gate = _rms_qkvg(h, norm_g, w_in.astype(BF16), head_dim=head_dim)
    og = _moba_attention(q, k, v, gate, batch=batch, seq=seq)
    return _out_proj_residual(h, og, w_out.astype(BF16), final_g)


def _row(p):
    return p.reshape(1, -1)


def kernel(x, l0_norm_g, l0_w_in, l0_w_out, l1_norm_g, l1_w_in, l1_conv_w, l1_conv_b, l1_ln_g, l1_ln_b, l1_w_out, l2_norm_g, l2_w_in, l2_conv_w, l2_conv_b, l2_w_rg, l2_b_rg, l2_w_ig, l2_b_ig, l2_lam, l2_w_out, l3_norm_g, l3_w_in, l3_w_out, final_g):
    batch, seq, d = x.shape
    h = x.reshape(batch * seq, d)

    h = _moba_layer(h, _row(l0_norm_g), l0_w_in, l0_w_out, batch=batch, seq=seq)

    h = _conformer_layer(h, _row(l1_norm_g), l1_w_in.astype(BF16), l1_conv_w, _row(l1_conv_b),
                         _row(l1_ln_g), _row(l1_ln_b), l1_w_out.astype(BF16), seq=seq)

    w_gates = jnp.concatenate([l2_w_rg, l2_w_ig], axis=-1).astype(BF16)
    h = _rglru_layer(h, _row(l2_norm_g), l2_w_in.astype(BF16), l2_conv_w, _row(l2_conv_b), w_gates,
                     _row(l2_b_rg), _row(l2_b_ig), _row(l2_lam), l2_w_out.astype(BF16), seq=seq)

    out = _moba_layer(h, _row(l3_norm_g), l3_w_in, l3_w_out, batch=batch, seq=seq, final_g=_row(final_g))
    return out.reshape(batch, seq, d)
```

```python
import functools

import jax
import jax.numpy as jnp
from jax import lax
from jax.experimental import pallas as pl
from jax.experimental.pallas import tpu as pltpu

F32 = jnp.float32
BF16 = jnp.bfloat16

EPS = 1e-6
NEG = -1e30

ATT_HEADS = 8
MOBA_BLOCK = 256
MOBA_TOPK = 3
LRU_HEADS = 10
LRU_C = 8.0

VMEM_LIMIT_BYTES = 56 * 1024 * 1024
LOG2_E = 1.4426950408889634


def _const_spec(shape):
    zeros = (0,) * len(shape)
    return pl.BlockSpec(shape, lambda *_: zeros, pipeline_mode=pl.Buffered(1))


def _params(n_axes):
    return pltpu.CompilerParams(
        dimension_semantics=("arbitrary",) * n_axes,
        vmem_limit_bytes=VMEM_LIMIT_BYTES,
    )


def _rmsnorm(x, g):
    return x * lax.rsqrt(jnp.mean(x * x, axis=-1, keepdims=True) + EPS) * g


def _sigmoid(x):
    return jax.nn.sigmoid(x)


def _silu(x):
    return x * jax.nn.sigmoid(x)


def _rms_qkvg_kernel(x_ref, g_ref, w_ref, q_ref, k_ref, v_ref, gate_ref, *, q_scale):
    xn = _rmsnorm(x_ref[...], g_ref[...]).astype(BF16)
    width = q_ref.shape[1]
    for c, ref in enumerate((q_ref, k_ref, v_ref, gate_ref)):
        acc = jnp.dot(xn, w_ref[:, c * width:(c + 1) * width], preferred_element_type=F32)
        if c == 0:
            acc = acc * q_scale
        ref[...] = acc.astype(ref.dtype)


def _rms_qkvg(h, g, w_in, *, head_dim, tm=512):
    n, d = h.shape
    width = w_in.shape[1] // 4
    out_bf = jax.ShapeDtypeStruct((n, width), BF16)
    row_spec = pl.BlockSpec((tm, width), lambda i: (i, 0))
    return pl.pallas_call(
        functools.partial(_rms_qkvg_kernel, q_scale=head_dim ** -0.5 * LOG2_E),
        grid=(n // tm,),
        in_specs=[
            pl.BlockSpec((tm, d), lambda i: (i, 0)),
            _const_spec((1, d)),
            _const_spec(w_in.shape),
        ],
        out_specs=[row_spec, row_spec, row_spec, row_spec],
        out_shape=[out_bf, out_bf, out_bf, jax.ShapeDtypeStruct((n, width), F32)],
        compiler_params=_params(1),
        name="rms_qkvg",
    )(h, g, w_in)


_TRANS_B = (((1,), (1,)), ((), ()))
_TRANS_A = (((0,), (0,)), ((), ()))


def _moba_attn_kernel(q_ref, k_ref, v_ref, gate_ref, o_ref, kmean_ref, sel_ref, s_ref, mx_ref, m_ref, l_ref,
                      acc_ref, *, blk, nb, topk, heads, dh):
    i = pl.program_id(1)

    @pl.when(i == 0)
    def _():
        for j in range(nb):
            kb = k_ref[j * blk:(j + 1) * blk, :].astype(F32)
            kmean_ref[j:j + 1, :] = jnp.mean(kb, axis=0, keepdims=True)

    def head_cols(hh):
        return slice(hh * dh, (hh + 1) * dh)

    def scores(hh, j):
        kj = k_ref[pl.ds(pl.multiple_of(j * blk, blk), blk), head_cols(hh)]
        return lax.dot_general(kj, q_ref[:, head_cols(hh)], _TRANS_B, preferred_element_type=F32)

    def weighted_values(hh, j, p):
        vj = v_ref[pl.ds(pl.multiple_of(j * blk, blk), blk), head_cols(hh)]
        return lax.dot_general(vj, p.astype(BF16), _TRANS_A, preferred_element_type=F32)

    n_iota = lax.broadcasted_iota(jnp.int32, (nb, blk), 0)
    past = n_iota < i
    k_idx = lax.broadcasted_iota(jnp.int32, (blk, blk), 0)
    q_idx = lax.broadcasted_iota(jnp.int32, (blk, blk), 1)
    causal = k_idx <= q_idx

    def stage1(hh, slot, s):
        s_ref[slot, hh] = s
        mx_ref[slot, hh] = jnp.max(s, axis=0, keepdims=True)

    def stage2(hh, slot, j):
        m_run = m_ref[hh]
        m_new = jnp.maximum(m_run, mx_ref[slot, hh])
        alpha = jnp.exp2(m_run - m_new)
        p = jnp.exp2(s_ref[slot, hh] - m_new)
        m_ref[hh] = m_new
        l_ref[hh] = alpha * l_ref[hh] + jnp.sum(p, axis=0, keepdims=True)
        acc_ref[hh] = alpha * acc_ref[hh] + weighted_values(hh, j, p)

    for hh in range(heads):
        q = q_ref[:, head_cols(hh)]
        km = kmean_ref[:, head_cols(hh)]
        km_hi = km.astype(BF16)
        km_lo = (km - km_hi.astype(F32)).astype(BF16)
        sc = (lax.dot_general(km_hi, q, _TRANS_B, preferred_element_type=F32)
              + lax.dot_general(km_lo, q, _TRANS_B, preferred_element_type=F32))
        sc = jnp.where(past, sc, NEG)
        rank = jnp.zeros((nb, blk), jnp.int32)
        for m in range(nb):
            row = sc[m:m + 1, :]
            beats = (row > sc) | ((row == sc) & (m < n_iota))
            rank = rank + beats.astype(jnp.int32)
        sel_ref[hh] = ((rank < topk) & past).astype(F32)

        stage1(hh, 0, jnp.where(causal, scores(hh, i), NEG))
        m_ref[hh] = jnp.full((1, blk), NEG, F32)
        l_ref[hh] = jnp.zeros((1, blk), F32)
        acc_ref[hh] = jnp.zeros((dh, blk), F32)

    def visit(t, cur):
        j_cur = jnp.where(t == 0, i, t - 1)
        for hh in range(heads):
            stage1(hh, 1 - cur, jnp.where(sel_ref[hh, pl.ds(t, 1), :] > 0.5, scores(hh, t), NEG))
            stage2(hh, cur, j_cur)

    def visit_pair(u, carry):
        visit(2 * u, 0)
        visit(2 * u + 1, 1)
        return carry

    lax.fori_loop(0, i // 2, visit_pair, 0)

    j_last = jnp.where(i == 0, i, i - 1)

    @pl.when(i % 2 == 1)
    def _():
        visit(i - 1, 0)
        for hh in range(heads):
            stage2(hh, 1, j_last)

    @pl.when(i % 2 == 0)
    def _():
        for hh in range(heads):
            stage2(hh, 0, j_last)

    for hh in range(heads):
        o = (acc_ref[hh] / l_ref[hh]).T
        o_ref[:, head_cols(hh)] = (o * _silu(gate_ref[:, head_cols(hh)])).astype(o_ref.dtype)


def _moba_attention(q, k, v, gate, *, batch, seq):
    n, width = q.shape
    heads = ATT_HEADS
    dh = width // heads
    blk = MOBA_BLOCK
    assert seq % blk == 0
    nb = seq // blk
    qspec = pl.BlockSpec((blk, width), lambda b, i: (b * nb + i, 0))
    kvspec = pl.BlockSpec((seq, width), lambda b, i: (b, 0))
    return pl.pallas_call(
        functools.partial(_moba_attn_kernel, blk=blk, nb=nb, topk=MOBA_TOPK, heads=heads, dh=dh),
        grid=(batch, nb),
        in_specs=[qspec, kvspec, kvspec, qspec],
        out_specs=qspec,
        out_shape=jax.ShapeDtypeStruct((n, width), BF16),
        scratch_shapes=[
            pltpu.VMEM((nb, width), F32),
            pltpu.VMEM((heads, nb, blk), F32),
            pltpu.VMEM((2, heads, blk, blk), F32),
            pltpu.VMEM((2, heads, 1, blk), F32),
            pltpu.VMEM((heads, 1, blk), F32),
            pltpu.VMEM((heads, 1, blk), F32),
            pltpu.VMEM((heads, dh, blk), F32),
        ],
        compiler_params=_params(2),
        name="moba_attention",
    )(q, k, v, gate)


def _out_proj_kernel(h_ref, a_ref, w_ref, *rest):
    o_ref = rest[-1]
    y = h_ref[...] + jnp.dot(a_ref[...], w_ref[...], preferred_element_type=F32)
    if len(rest) == 2:
        y = _rmsnorm(y, rest[0][...])
    o_ref[...] = y


def _out_proj_residual(h, a, w_out, final_g=None, *, tm=512):
    n, d = h.shape
    kdim = a.shape[1]
    in_specs = [
        pl.BlockSpec((tm, d), lambda i: (i, 0)),
        pl.BlockSpec((tm, kdim), lambda i: (i, 0)),
        _const_spec(w_out.shape),
    ]
    args = [h, a, w_out]
    if final_g is not None:
        in_specs.append(_const_spec((1, d)))
        args.append(final_g)
    return pl.pallas_call(
        _out_proj_kernel,
        grid=(n // tm,),
        in_specs=in_specs,
        out_specs=pl.BlockSpec((tm, d), lambda i: (i, 0)),
        out_shape=jax.ShapeDtypeStruct((n, d), F32),
        compiler_params=_params(1),
        name="out_proj_residual",
    )(*args)


_CONV_HIST = 32
_SUBLANES = 8
_LANES = 128
_CONV_ROWS = 64


def _conformer_kernel(h_ref, g_ref, w_in_ref, cw_ref, cb_ref, lng_ref, lnb_ref, w_out_ref, o_ref,
                      yr_ref, conv_ref, *, tiles_per_seq, kw):
    tm, c = o_ref.shape
    hist = _CONV_HIST

    @pl.when(pl.program_id(0) % tiles_per_seq == 0)
    def _():
        yr_ref[0, 0:hist, :] = jnp.zeros((hist, c), F32)

    h = h_ref[...]
    xn = _rmsnorm(h, g_ref[...]).astype(BF16)
    a = jnp.dot(xn, w_in_ref[:, 0:c], preferred_element_type=F32)
    b = jnp.dot(xn, w_in_ref[:, c:2 * c], preferred_element_type=F32)
    yr_ref[0, hist:hist + tm, :] = a * _sigmoid(b)
    span = hist + tm - _SUBLANES
    for r in range(1, _SUBLANES):
        yr_ref[r, 0:span, :] = yr_ref[0, r:r + span, :]

    base = hist - (kw - 1)

    def conv_rows(ci, carry):
        t0 = ci * _CONV_ROWS
        for lc in range(c // _LANES):
            cols = slice(lc * _LANES, (lc + 1) * _LANES)
            acc = jnp.broadcast_to(cb_ref[:, cols], (_CONV_ROWS, _LANES))
            for j in range(kw):
                a8, r = divmod(base + j, _SUBLANES)
                start = pl.multiple_of(t0 + a8 * _SUBLANES, _SUBLANES)
                acc = acc + cw_ref[j:j + 1, cols] * yr_ref[r, pl.ds(start, _CONV_ROWS), cols]
            conv_ref[pl.ds(pl.multiple_of(t0, _CONV_ROWS), _CONV_ROWS), cols] = acc
        return carry

    lax.fori_loop(0, tm // _CONV_ROWS, conv_rows, 0)
    yr_ref[0, 0:hist, :] = yr_ref[0, tm:tm + hist, :]

    acc = conv_ref[...]
    mu = jnp.mean(acc, axis=-1, keepdims=True)
    cen = acc - mu
    var = jnp.mean(cen * cen, axis=-1, keepdims=True)
    y = cen * lax.rsqrt(var + EPS) * lng_ref[...] + lnb_ref[...]
    gate = jnp.dot(xn, w_in_ref[:, 2 * c:3 * c], preferred_element_type=F32)
    z = (_silu(y) * _silu(gate)).astype(BF16)
    o_ref[...] = h + jnp.dot(z, w_out_ref[...], preferred_element_type=F32)


def _conformer_layer(h, g, w_in, conv_w, conv_b, ln_g, ln_b, w_out, *, seq, tm=256):
    n, d = h.shape
    c = w_out.shape[0]
    kw = conv_w.shape[0]
    assert kw - 1 <= _CONV_HIST <= tm and seq % tm == 0 and tm % _CONV_ROWS == 0 and c % _LANES == 0
    return pl.pallas_call(
        functools.partial(_conformer_kernel, tiles_per_seq=seq // tm, kw=kw),
        grid=(n // tm,),
        in_specs=[
            pl.BlockSpec((tm, d), lambda i: (i, 0)),
            _const_spec((1, d)),
            _const_spec(w_in.shape),
            _const_spec(conv_w.shape),
            _const_spec((1, c)),
            _const_spec((1, c)),
            _const_spec((1, c)),
            _const_spec(w_out.shape),
        ],
        out_specs=pl.BlockSpec((tm, d), lambda i: (i, 0)),
        out_shape=jax.ShapeDtypeStruct((n, d), F32),
        scratch_shapes=[pltpu.VMEM((_SUBLANES, _CONV_HIST + tm, c), F32), pltpu.VMEM((tm, c), F32)],
        compiler_params=_params(1),
        name="conformer_layer",
    )(h, g, w_in, conv_w, conv_b, ln_g, ln_b, w_out)


_LRU_HIST = 8


def _rglru_kernel(h_ref, g_ref, w_in_ref, cw_ref, cb_ref, w_gates_ref, b_rg_ref, b_ig_ref, lam_ref,
                  w_out_ref, o_ref, xbuf_ref, a_ref, b_ref, hs_ref, state_ref, *, tiles_per_seq, kw, heads):
    tm = o_ref.shape[0]
    w = w_out_ref.shape[0]
    hd = w // heads
    hist = _LRU_HIST

    @pl.when(pl.program_id(0) % tiles_per_seq == 0)
    def _():
        xbuf_ref[0:hist, :] = jnp.zeros((hist, w), F32)
        state_ref[...] = jnp.zeros_like(state_ref)

    h = h_ref[...]
    xn = _rmsnorm(h, g_ref[...]).astype(BF16)
    xbuf_ref[hist:hist + tm, :] = jnp.dot(xn, w_in_ref[:, 0:w], preferred_element_type=F32)

    base = hist - (kw - 1)
    xc = cb_ref[...] + cw_ref[0:1, :] * xbuf_ref[base:base + tm, :]
    for j in range(1, kw):
        xc = xc + cw_ref[j:j + 1, :] * xbuf_ref[base + j:base + j + tm, :]
    xbuf_ref[0:hist, :] = xbuf_ref[tm:tm + hist, :]

    neg_lam = -lam_ref[...]
    c_softplus = LRU_C * (jnp.maximum(neg_lam, 0.0) + jnp.log1p(jnp.exp(-jnp.abs(neg_lam))))
    xc_bf = xc.astype(BF16)
    for hh in range(heads):
        cols = slice(hh * hd, (hh + 1) * hd)
        gates = jnp.dot(xc_bf[:, cols], w_gates_ref[hh], preferred_element_type=F32)
        r = _sigmoid(gates[:, 0:hd] + b_rg_ref[:, cols])
        ig = _sigmoid(gates[:, hd:2 * hd] + b_ig_ref[:, cols])
        log_a = -(r * c_softplus[:, cols])
        a = jnp.exp(log_a)
        a_ref[:, cols] = a
        one_minus_a2 = -jnp.tanh(log_a) * (a * a + 1.0)
        b_ref[:, cols] = jnp.sqrt(one_minus_a2) * (ig * xc[:, cols])

    row = lax.broadcasted_iota(jnp.int32, (_SUBLANES, w), 0)

    def scan_group(r, carry):
        r0 = pl.multiple_of(r * _SUBLANES, _SUBLANES)
        av = a_ref[pl.ds(r0, _SUBLANES), :]
        bv = b_ref[pl.ds(r0, _SUBLANES), :]
        for s in (1, 2, 4):
            a_prev = jnp.where(row >= s, pltpu.roll(av, s, 0), 1.0)
            b_prev = jnp.where(row >= s, pltpu.roll(bv, s, 0), 0.0)
            bv = av * b_prev + bv
            av = av * a_prev
        hv = av * carry + bv
        hs_ref[pl.ds(r0, _SUBLANES), :] = hv
        return hv[_SUBLANES - 1:_SUBLANES, :]

    state_ref[...] = lax.fori_loop(0, tm // _SUBLANES, scan_group, state_ref[...])

    gate = jnp.dot(xn, w_in_ref[:, w:2 * w], preferred_element_type=F32)
    y = (hs_ref[...] * _silu(gate)).astype(BF16)
    o_ref[...] = h + jnp.dot(y, w_out_ref[...], preferred_element_type=F32)


def _rglru_layer(h, g, w_in, conv_w, conv_b, w_gates, b_rg, b_ig, lam, w_out, *, seq, tm=256):
    n, d = h.shape
    w = w_out.shape[0]
    kw = conv_w.shape[0]
    assert kw - 1 <= _LRU_HIST <= tm and seq % tm == 0 and tm % _SUBLANES == 0
    return pl.pallas_call(
        functools.partial(_rglru_kernel, tiles_per_seq=seq // tm, kw=kw, heads=LRU_HEADS),
        grid=(n // tm,),
        in_specs=[
            pl.BlockSpec((tm, d), lambda i: (i, 0)),
            _const_spec((1, d)),
            _const_spec(w_in.shape),
            _const_spec(conv_w.shape),
            _const_spec((1, w)),
            _const_spec(w_gates.shape),
            _const_spec((1, w)),
            _const_spec((1, w)),
            _const_spec((1, w)),
            _const_spec(w_out.shape),
        ],
        out_specs=pl.BlockSpec((tm, d), lambda i: (i, 0)),
        out_shape=jax.ShapeDtypeStruct((n, d), F32),
        scratch_shapes=[
            pltpu.VMEM((_LRU_HIST + tm, w), F32),
            pltpu.VMEM((tm, w), F32),
            pltpu.VMEM((tm, w), F32),
            pltpu.VMEM((tm, w), F32),
            pltpu.VMEM((1, w), F32),
        ],
        compiler_params=_params(1),
        name="rglru_layer",
    )(h, g, w_in, conv_w, conv_b, w_gates, b_rg, b_ig, lam, w_out)


def _moba_layer(h, norm_g, w_in, w_out, *, batch, seq, final_g=None):
    head_dim = w_out.shape[0] // ATT_HEADS
    q, k, v, gate = _rms_qkvg(h, norm_g, w_in.astype(BF16), head_dim=head_dim)
    og = _moba_attention(q, k, v, gate, batch=batch, seq=seq)
    return _out_proj_residual(h, og, w_out.astype(BF16), final_g)


def _row(p):
    return p.reshape(1, -1)


def kernel(x, l0_norm_g, l0_w_in, l0_w_out, l1_norm_g, l1_w_in, l1_conv_w, l1_conv_b, l1_ln_g, l1_ln_b, l1_w_out, l2_norm_g, l2_w_in, l2_conv_w, l2_conv_b, l2_w_rg, l2_b_rg, l2_w_ig, l2_b_ig, l2_lam, l2_w_out, l3_norm_g, l3_w_in, l3_w_out, final_g):
    batch, seq, d = x.shape
    h = x.reshape(batch * seq, d)

    h = _moba_layer(h, _row(l0_norm_g), l0_w_in, l0_w_out, batch=batch, seq=seq)

    h = _conformer_layer(h, _row(l1_norm_g), l1_w_in.astype(BF16), l1_conv_w, _row(l1_conv_b),
                         _row(l1_ln_g), _row(l1_ln_b), l1_w_out.astype(BF16), seq=seq)

    w_gates = jnp.concatenate([l2_w_rg, l2_w_ig], axis=-1).astype(BF16)
    h = _rglru_layer(h, _row(l2_norm_g), l2_w_in.astype(BF16), l2_conv_w, _row(l2_conv_b), w_gates,
                     _row(l2_b_rg), _row(l2_b_ig), _row(l2_lam), l2_w_out.astype(BF16), seq=seq)

    out = _moba_layer(h, _row(l3_norm_g), l3_w_in, l3_w_out, batch=batch, seq=seq, final_g=_row(final_g))
    return out.reshape(batch, seq, d)
```

```python
import functools

import jax
import jax.numpy as jnp
from jax import lax
from jax.experimental import pallas as pl
from jax.experimental.pallas import tpu as pltpu

F32 = jnp.float32
BF16 = jnp.bfloat16

EPS = 1e-6
NEG = -1e30

ATT_HEADS = 8
MOBA_BLOCK = 256
MOBA_TOPK = 3
LRU_HEADS = 10
LRU_C = 8.0

VMEM_LIMIT_BYTES = 56 * 1024 * 1024
LOG2_E = 1.4426950408889634


def _const_spec(shape):
    zeros = (0,) * len(shape)
    return pl.BlockSpec(shape, lambda *_: zeros, pipeline_mode=pl.Buffered(1))


def _params(n_axes):
    return pltpu.CompilerParams(
        dimension_semantics=("arbitrary",) * n_axes,
        vmem_limit_bytes=VMEM_LIMIT_BYTES,
    )


def _rmsnorm(x, g):
    return x * lax.rsqrt(jnp.mean(x * x, axis=-1, keepdims=True) + EPS) * g


def _sigmoid(x):
    return 0.5 * jnp.tanh(0.5 * x) + 0.5


def _silu(x):
    u = 0.5 * x
    return u * jnp.tanh(u) + u


def _rms_qkvg_kernel(x_ref, g_ref, w_ref, q_ref, k_ref, v_ref, gate_ref, *, q_scale):
    xn = _rmsnorm(x_ref[...], g_ref[...]).astype(BF16)
    width = q_ref.shape[1]
    for c, ref in enumerate((q_ref, k_ref, v_ref, gate_ref)):
        acc = jnp.dot(xn, w_ref[:, c * width:(c + 1) * width], preferred_element_type=F32)
        if c == 0:
            acc = acc * q_scale
        ref[...] = acc.astype(ref.dtype)


def _rms_qkvg(h, g, w_in, *, head_dim, tm=512):
    n, d = h.shape
    width = w_in.shape[1] // 4
    out_bf = jax.ShapeDtypeStruct((n, width), BF16)
    row_spec = pl.BlockSpec((tm, width), lambda i: (i, 0))
    return pl.pallas_call(
        functools.partial(_rms_qkvg_kernel, q_scale=head_dim ** -0.5 * LOG2_E),
        grid=(n // tm,),
        in_specs=[
            pl.BlockSpec((tm, d), lambda i: (i, 0)),
            _const_spec((1, d)),
            _const_spec(w_in.shape),
        ],
        out_specs=[row_spec, row_spec, row_spec, row_spec],
        out_shape=[out_bf, out_bf, out_bf, jax.ShapeDtypeStruct((n, width), F32)],
        compiler_params=_params(1),
        name="rms_qkvg",
    )(h, g, w_in)


_TRANS_B = (((1,), (1,)), ((), ()))
_TRANS_A = (((0,), (0,)), ((), ()))


_SUM_ROWS = 16


def _moba_attn_kernel(q_ref, k_ref, v_ref, gate_ref, o_ref, kmean_ref, vt_ref, bias_ref, qa_ref, s_ref, mx_ref,
                      m_ref, acc_ref, *, blk, nb, topk, heads, dh):
    i = pl.program_id(1)

    def head_cols(hh):
        return slice(hh * dh, (hh + 1) * dh)

    @pl.when(i == 0)
    def _():
        lane = lax.broadcasted_iota(jnp.int32, (blk, dh), 1)

        def prep_block(j, carry):
            rows = pl.ds(pl.multiple_of(j * blk, blk), blk)
            kmean_ref[pl.ds(j, 1), :] = jnp.mean(k_ref[rows, :].astype(F32), axis=0, keepdims=True)
            bias_ref[j] = jnp.where(lane == j, NEG, 0.0).astype(BF16)
            for hh in range(heads):
                vt_ref[hh, j, 0:dh, :] = v_ref[rows, head_cols(hh)].astype(F32).T.astype(BF16)
                vt_ref[hh, j, dh:dh + _SUM_ROWS, :] = jnp.ones((_SUM_ROWS, blk), BF16)
            return carry

        lax.fori_loop(0, nb, prep_block, 0)

    def scores(hh, j):
        kj = k_ref[pl.ds(pl.multiple_of(j * blk, blk), blk), head_cols(hh)]
        k_aug = jnp.concatenate([kj, bias_ref[j]], axis=1)
        return lax.dot_general(k_aug, qa_ref[hh], _TRANS_B, preferred_element_type=F32)

    n_iota = lax.broadcasted_iota(jnp.int32, (nb, blk), 0)
    past = n_iota < i
    k_idx = lax.broadcasted_iota(jnp.int32, (blk, blk), 0)
    q_idx = lax.broadcasted_iota(jnp.int32, (blk, blk), 1)
    causal = k_idx <= q_idx

    def stage1(hh, slot, s):
        s_ref[slot, hh] = s
        mx_ref[slot, hh] = jnp.max(s, axis=0, keepdims=True)

    def stage2(hh, slot, j):
        m_run = m_ref[hh]
        m_new = jnp.maximum(m_run, mx_ref[slot, hh])
        alpha = jnp.exp2(m_run - m_new)
        p = jnp.exp2(s_ref[slot, hh] - m_new).astype(BF16)
        m_ref[hh] = m_new
        acc_ref[hh] = alpha * acc_ref[hh] + jnp.dot(vt_ref[hh, j], p, preferred_element_type=F32)

    for hh in range(heads):
        q = q_ref[:, head_cols(hh)]
        km = kmean_ref[:, head_cols(hh)]
        km_hi = km.astype(BF16)
        km_lo = (km - km_hi.astype(F32)).astype(BF16)
        sc = (lax.dot_general(km_hi, q, _TRANS_B, preferred_element_type=F32)
              + lax.dot_general(km_lo, q, _TRANS_B, preferred_element_type=F32))
        sc = jnp.where(past, sc, NEG)
        rank = jnp.zeros((nb, blk), jnp.int32)
        for m in range(nb):
            row = sc[m:m + 1, :]
            beats = (row > sc) | ((row == sc) & (m < n_iota))
            rank = rank + beats.astype(jnp.int32)
        unselected = jnp.where(past & (rank >= topk), 1.0, 0.0)
        unselected = jnp.concatenate([unselected, jnp.zeros((dh - nb, blk), F32)], axis=0)
        qa_ref[hh, :, 0:dh] = q
        qa_ref[hh, :, dh:2 * dh] = unselected.T.astype(BF16)

    for hh in range(heads):
        stage1(hh, 0, jnp.where(causal, scores(hh, i), NEG))
        m_ref[hh] = jnp.full((1, blk), NEG, F32)
        acc_ref[hh] = jnp.zeros((dh + _SUM_ROWS, blk), F32)

    def visit(t, cur):
        j_cur = jnp.where(t == 0, i, t - 1)
        for hh in range(heads):
            stage1(hh, 1 - cur, scores(hh, t))
            stage2(hh, cur, j_cur)

    def visit_pair(u, carry):
        visit(2 * u, 0)
        visit(2 * u + 1, 1)
        return carry

    lax.fori_loop(0, i // 2, visit_pair, 0)

    j_last = jnp.where(i == 0, i, i - 1)

    @pl.when(i % 2 == 1)
    def _():
        visit(i - 1, 0)
        for hh in range(heads):
            stage2(hh, 1, j_last)

    @pl.when(i % 2 == 0)
    def _():
        for hh in range(heads):
            stage2(hh, 0, j_last)

    for hh in range(heads):
        o = (acc_ref[hh, 0:dh, :] / acc_ref[hh, dh:dh + 1, :]).T
        o_ref[:, head_cols(hh)] = (o * _silu(gate_ref[:, head_cols(hh)])).astype(o_ref.dtype)


def _moba_attention(q, k, v, gate, *, batch, seq):
    n, width = q.shape
    heads = ATT_HEADS
    dh = width // heads
    blk = MOBA_BLOCK
    assert seq % blk == 0
    nb = seq // blk
    assert nb <= dh
    qspec = pl.BlockSpec((blk, width), lambda b, i: (b * nb + i, 0))
    kspec = pl.BlockSpec((seq, width), lambda b, i: (b, 0))
    vspec = pl.BlockSpec((seq, width), lambda b, i: (b, 0), pipeline_mode=pl.Buffered(1))
    return pl.pallas_call(
        functools.partial(_moba_attn_kernel, blk=blk, nb=nb, topk=MOBA_TOPK, heads=heads, dh=dh),
        grid=(batch, nb),
        in_specs=[qspec, kspec, vspec, qspec],
        out_specs=qspec,
        out_shape=jax.ShapeDtypeStruct((n, width), BF16),
        scratch_shapes=[
            pltpu.VMEM((nb, width), F32),
            pltpu.VMEM((heads, nb, dh + _SUM_ROWS, blk), BF16),
            pltpu.VMEM((nb, blk, dh), BF16),
            pltpu.VMEM((heads, blk, 2 * dh), BF16),
            pltpu.VMEM((2, heads, blk, blk), F32),
            pltpu.VMEM((2, heads, 1, blk), F32),
            pltpu.VMEM((heads, 1, blk), F32),
            pltpu.VMEM((heads, dh + _SUM_ROWS, blk), F32),
        ],
        compiler_params=_params(2),
        name="moba_attention",
    )(q, k, v, gate)


def _out_proj_kernel(h_ref, a_ref, w_ref, *rest):
    o_ref = rest[-1]
    y = h_ref[...] + jnp.dot(a_ref[...], w_ref[...], preferred_element_type=F32)
    if len(rest) == 2:
        y = _rmsnorm(y, rest[0][...])
    o_ref[...] = y


def _out_proj_residual(h, a, w_out, final_g=None, *, tm=512):
    n, d = h.shape
    kdim = a.shape[1]
    in_specs = [
        pl.BlockSpec((tm, d), lambda i: (i, 0)),
        pl.BlockSpec((tm, kdim), lambda i: (i, 0)),
        _const_spec(w_out.shape),
    ]
    args = [h, a, w_out]
    if final_g is not None:
        in_specs.append(_const_spec((1, d)))
        args.append(final_g)
    return pl.pallas_call(
        _out_proj_kernel,
        grid=(n // tm,),
        in_specs=in_specs,
        out_specs=pl.BlockSpec((tm, d), lambda i: (i, 0)),
        out_shape=jax.ShapeDtypeStruct((n, d), F32),
        compiler_params=_params(1),
        name="out_proj_residual",
    )(*args)


_CONV_HIST = 32
_SUBLANES = 8
_LANES = 128
_CONV_ROWS = 64


def _conformer_kernel(h_ref, g_ref, w_in_ref, cw_ref, cb_ref, lng_ref, lnb_ref, w_out_ref, o_ref,
                      yr_ref, conv_ref, *, tiles_per_seq, kw):
    tm, c = o_ref.shape
    hist = _CONV_HIST

    @pl.when(pl.program_id(0) % tiles_per_seq == 0)
    def _():
        yr_ref[0, 0:hist, :] = jnp.zeros((hist, c), F32)

    h = h_ref[...]
    xn = _rmsnorm(h, g_ref[...]).astype(BF16)
    a = jnp.dot(xn, w_in_ref[:, 0:c], preferred_element_type=F32)
    b = jnp.dot(xn, w_in_ref[:, c:2 * c], preferred_element_type=F32)
    yr_ref[0, hist:hist + tm, :] = a * _sigmoid(b)
    span = hist + tm - _SUBLANES
    for r in range(1, _SUBLANES):
        yr_ref[r, 0:span, :] = yr_ref[0, r:r + span, :]

    base = hist - (kw - 1)

    for t0 in range(0, tm, _CONV_ROWS):
        for lc in range(c // _LANES):
            cols = slice(lc * _LANES, (lc + 1) * _LANES)
            acc = jnp.broadcast_to(cb_ref[:, cols], (_CONV_ROWS, _LANES))
            for j in range(kw):
                a8, r = divmod(base + j, _SUBLANES)
                start = t0 + a8 * _SUBLANES
                acc = acc + cw_ref[j:j + 1, cols] * yr_ref[r, start:start + _CONV_ROWS, cols]
            conv_ref[t0:t0 + _CONV_ROWS, cols] = acc
    yr_ref[0, 0:hist, :] = yr_ref[0, tm:tm + hist, :]

    acc = conv_ref[...]
    mu = jnp.mean(acc, axis=-1, keepdims=True)
    cen = acc - mu
    var = jnp.mean(cen * cen, axis=-1, keepdims=True)
    y = cen * lax.rsqrt(var + EPS) * lng_ref[...] + lnb_ref[...]
    gate = jnp.dot(xn, w_in_ref[:, 2 * c:3 * c], preferred_element_type=F32)
    z = (_silu(y) * _silu(gate)).astype(BF16)
    o_ref[...] = h + jnp.dot(z, w_out_ref[...], preferred_element_type=F32)


def _conformer_layer(h, g, w_in, conv_w, conv_b, ln_g, ln_b, w_out, *, seq, tm=256):
    n, d = h.shape
    c = w_out.shape[0]
    kw = conv_w.shape[0]
    assert kw - 1 <= _CONV_HIST <= tm and seq % tm == 0 and tm % _CONV_ROWS == 0 and c % _LANES == 0
    return pl.pallas_call(
        functools.partial(_conformer_kernel, tiles_per_seq=seq // tm, kw=kw),
        grid=(n // tm,),
        in_specs=[
            pl.BlockSpec((tm, d), lambda i: (i, 0)),
            _const_spec((1, d)),
            _const_spec(w_in.shape),
            _const_spec(conv_w.shape),
            _const_spec((1, c)),
            _const_spec((1, c)),
            _const_spec((1, c)),
            _const_spec(w_out.shape),
        ],
        out_specs=pl.BlockSpec((tm, d), lambda i: (i, 0)),
        out_shape=jax.ShapeDtypeStruct((n, d), F32),
        scratch_shapes=[pltpu.VMEM((_SUBLANES, _CONV_HIST + tm, c), F32), pltpu.VMEM((tm, c), F32)],
        compiler_params=_params(1),
        name="conformer_layer",
    )(h, g, w_in, conv_w, conv_b, ln_g, ln_b, w_out)


_LRU_HIST = 8


def _rglru_kernel(h_ref, g_ref, w_in_ref, cw_ref, cb_ref, w_gates_ref, b_rg_ref, b_ig_ref, lam_ref,
                  w_out_ref, o_ref, xbuf_ref, a_ref, b_ref, hs_ref, state_ref, *, tiles_per_seq, kw, heads):
    tm = o_ref.shape[0]
    w = w_out_ref.shape[0]
    hd = w // heads
    hist = _LRU_HIST

    @pl.when(pl.program_id(0) % tiles_per_seq == 0)
    def _():
        xbuf_ref[0:hist, :] = jnp.zeros((hist, w), F32)
        state_ref[...] = jnp.zeros_like(state_ref)

    h = h_ref[...]
    xn = _rmsnorm(h, g_ref[...]).astype(BF16)
    xbuf_ref[hist:hist + tm, :] = jnp.dot(xn, w_in_ref[:, 0:w], preferred_element_type=F32)

    base = hist - (kw - 1)
    xc = cb_ref[...] + cw_ref[0:1, :] * xbuf_ref[base:base + tm, :]
    for j in range(1, kw):
        xc = xc + cw_ref[j:j + 1, :] * xbuf_ref[base + j:base + j + tm, :]
    xbuf_ref[0:hist, :] = xbuf_ref[tm:tm + hist, :]

    neg_lam = -lam_ref[...]
    c_softplus = LRU_C * (jnp.maximum(neg_lam, 0.0) + jnp.log1p(jnp.exp(-jnp.abs(neg_lam))))
    xc_bf = xc.astype(BF16)
    for hh in range(heads):
        cols = slice(hh * hd, (hh + 1) * hd)
        gates = jnp.dot(xc_bf[:, cols], w_gates_ref[hh], preferred_element_type=F32)
        r = _sigmoid(gates[:, 0:hd] + b_rg_ref[:, cols])
        ig = _sigmoid(gates[:, hd:2 * hd] + b_ig_ref[:, cols])
        log_a = -(r * c_softplus[:, cols])
        a = jnp.exp(log_a)
        a_ref[:, cols] = a
        one_minus_a2 = -jnp.tanh(log_a) * (a * a + 1.0)
        b_ref[:, cols] = jnp.sqrt(one_minus_a2) * (ig * xc[:, cols])

    row = lax.broadcasted_iota(jnp.int32, (_SUBLANES, w), 0)

    def scan_group(r, carry):
        r0 = pl.multiple_of(r * _SUBLANES, _SUBLANES)
        av = a_ref[pl.ds(r0, _SUBLANES), :]
        bv = b_ref[pl.ds(r0, _SUBLANES), :]
        for s in (1, 2, 4):
            a_prev = jnp.where(row >= s, pltpu.roll(av, s, 0), 1.0)
            b_prev = jnp.where(row >= s, pltpu.roll(bv, s, 0), 0.0)
            bv = av * b_prev + bv
            av = av * a_prev
        hv = av * carry + bv
        hs_ref[pl.ds(r0, _SUBLANES), :] = hv
        return hv[_SUBLANES - 1:_SUBLANES, :]

    state_ref[...] = lax.fori_loop(0, tm // _SUBLANES, scan_group, state_ref[...])

    gate = jnp.dot(xn, w_in_ref[:, w:2 * w], preferred_element_type=F32)
    y = (hs_ref[...] * _silu(gate)).astype(BF16)
    o_ref[...] = h + jnp.dot(y, w_out_ref[...], preferred_element_type=F32)


def _rglru_layer(h, g, w_in, conv_w, conv_b, w_gates, b_rg, b_ig, lam, w_out, *, seq, tm=256):
    n, d = h.shape
    w = w_out.shape[0]
    kw = conv_w.shape[0]
    assert kw - 1 <= _LRU_HIST <= tm and seq % tm == 0 and tm % _SUBLANES == 0
    return pl.pallas_call(
        functools.partial(_rglru_kernel, tiles_per_seq=seq // tm, kw=kw, heads=LRU_HEADS),
        grid=(n // tm,),
        in_specs=[
            pl.BlockSpec((tm, d), lambda i: (i, 0)),
            _const_spec((1, d)),
            _const_spec(w_in.shape),
            _const_spec(conv_w.shape),
            _const_spec((1, w)),
            _const_spec(w_gates.shape),
            _const_spec((1, w)),
            _const_spec((1, w)),
            _const_spec((1, w)),
            _const_spec(w_out.shape),
        ],
        out_specs=pl.BlockSpec((tm, d), lambda i: (i, 0)),
        out_shape=jax.ShapeDtypeStruct((n, d), F32),
        scratch_shapes=[
            pltpu.VMEM((_LRU_HIST + tm, w), F32),
            pltpu.VMEM((tm, w), F32),
            pltpu.VMEM((tm, w), F32),
            pltpu.VMEM((tm, w), F32),
            pltpu.VMEM((1, w), F32),
        ],
        compiler_params=_params(1),
        name="rglru_layer",
    )(h, g, w_in, conv_w, conv_b, w_gates, b_rg, b_ig, lam, w_out)


def _moba_layer(h, norm_g, w_in, w_out, *, batch, seq, final_g=None):
    head_dim = w_out.shape[0] // ATT_HEADS
    q, k, v, gate = _rms_qkvg(h, norm_g, w_in.astype(BF16), head_dim=head_dim)
    og = _moba_attention(q, k, v, gate, batch=batch, seq=seq)
    return _out_proj_residual(h, og, w_out.astype(BF16), final_g)


def _row(p):
    return p.reshape(1, -1)


def kernel(x, l0_norm_g, l0_w_in, l0_w_out, l1_norm_g, l1_w_in, l1_conv_w, l1_conv_b, l1_ln_g, l1_ln_b, l1_w_out, l2_norm_g, l2_w_in, l2_conv_w, l2_conv_b, l2_w_rg, l2_b_rg, l2_w_ig, l2_b_ig, l2_lam, l2_w_out, l3_norm_g, l3_w_in, l3_w_out, final_g):
    batch, seq, d = x.shape
    h = x.reshape(batch * seq, d)

    h = _moba_layer(h, _row(l0_norm_g), l0_w_in, l0_w_out, batch=batch, seq=seq)

    h = _conformer_layer(h, _row(l1_norm_g), l1_w_in.astype(BF16), l1_conv_w, _row(l1_conv_b),
                         _row(l1_ln_g), _row(l1_ln_b), l1_w_out.astype(BF16), seq=seq)

    w_gates = jnp.concatenate([l2_w_rg, l2_w_ig], axis=-1).astype(BF16)
    h = _rglru_layer(h, _row(l2_norm_g), l2_w_in.astype(BF16), l2_conv_w, _row(l2_conv_b), w_gates,
                     _row(l2_b_rg), _row(l2_b_ig), _row(l2_lam), l2_w_out.astype(BF16), seq=seq)

    out = _moba_layer(h, _row(l3_norm_g), l3_w_in, l3_w_out, batch=batch, seq=seq, final_g=_row(final_g))
    return out.reshape(batch, seq, d)
```

```python
import functools

import jax
import jax.numpy as jnp
from jax import lax
from jax.experimental import pallas as pl
from jax.experimental.pallas import tpu as pltpu

F32 = jnp.float32
BF16 = jnp.bfloat16

EPS = 1e-6
NEG = -1e30

ATT_HEADS = 8
MOBA_BLOCK = 256
MOBA_TOPK = 3
LRU_HEADS = 10
LRU_C = 8.0

VMEM_LIMIT_BYTES = 56 * 1024 * 1024
LOG2_E = 1.4426950408889634


def _const_spec(shape):
    zeros = (0,) * len(shape)
    return pl.BlockSpec(shape, lambda *_: zeros, pipeline_mode=pl.Buffered(1))


def _params(n_axes):
    return pltpu.CompilerParams(
        dimension_semantics=("arbitrary",) * n_axes,
        vmem_limit_bytes=VMEM_LIMIT_BYTES,
    )


def _rmsnorm(x, g):
    return x * lax.rsqrt(jnp.mean(x * x, axis=-1, keepdims=True) + EPS) * g


def _sigmoid(x):
    return 0.5 * jnp.tanh(0.5 * x) + 0.5


def _silu(x):
    u = 0.5 * x
    return u * jnp.tanh(u) + u


def _rms_qkvg_kernel(x_ref, g_ref, w_ref, q_ref, k_ref, v_ref, gate_ref, *, q_scale):
    xn = _rmsnorm(x_ref[...], g_ref[...]).astype(BF16)
    width = q_ref.shape[1]
    for c, ref in enumerate((q_ref, k_ref, v_ref, gate_ref)):
        acc = jnp.dot(xn, w_ref[:, c * width:(c + 1) * width], preferred_element_type=F32)
        if c == 0:
            acc = acc * q_scale
        ref[...] = acc.astype(ref.dtype)


def _rms_qkvg(h, g, w_in, *, head_dim, tm=512):
    n, d = h.shape
    width = w_in.shape[1] // 4
    out_bf = jax.ShapeDtypeStruct((n, width), BF16)
    row_spec = pl.BlockSpec((tm, width), lambda i: (i, 0))
    return pl.pallas_call(
        functools.partial(_rms_qkvg_kernel, q_scale=head_dim ** -0.5 * LOG2_E),
        grid=(n // tm,),
        in_specs=[
            pl.BlockSpec((tm, d), lambda i: (i, 0)),
            _const_spec((1, d)),
            _const_spec(w_in.shape),
        ],
        out_specs=[row_spec, row_spec, row_spec, row_spec],
        out_shape=[out_bf, out_bf, out_bf, jax.ShapeDtypeStruct((n, width), F32)],
        compiler_params=_params(1),
        name="rms_qkvg",
    )(h, g, w_in)


_TRANS_B = (((1,), (1,)), ((), ()))
_TRANS_A = (((0,), (0,)), ((), ()))


_SUM_ROWS = 16


def _moba_attn_kernel(q_ref, k_ref, v_ref, gate_ref, h_ref, w_out_ref, final_g_ref, o_ref, kmean_ref, vt_ref,
                      bias_ref, qa_ref, s_ref, mx_ref, m_ref, acc_ref, og_ref, *, blk, nb, topk, heads, dh):
    i = pl.program_id(1)

    def head_cols(hh):
        return slice(hh * dh, (hh + 1) * dh)

    @pl.when(i == 0)
    def _():
        lane = lax.broadcasted_iota(jnp.int32, (blk, dh), 1)

        def prep_block(j, carry):
            rows = pl.ds(pl.multiple_of(j * blk, blk), blk)
            kmean_ref[pl.ds(j, 1), :] = jnp.mean(k_ref[rows, :].astype(F32), axis=0, keepdims=True)
            bias_ref[j] = jnp.where(lane == j, NEG, 0.0).astype(BF16)
            for hh in range(heads):
                vt_ref[hh, j, 0:dh, :] = v_ref[rows, head_cols(hh)].astype(F32).T.astype(BF16)
                vt_ref[hh, j, dh:dh + _SUM_ROWS, :] = jnp.ones((_SUM_ROWS, blk), BF16)
            return carry

        lax.fori_loop(0, nb, prep_block, 0)

    def scores(hh, j):
        kj = k_ref[pl.ds(pl.multiple_of(j * blk, blk), blk), head_cols(hh)]
        k_aug = jnp.concatenate([kj, bias_ref[j]], axis=1)
        return lax.dot_general(k_aug, qa_ref[hh], _TRANS_B, preferred_element_type=F32)

    n_iota = lax.broadcasted_iota(jnp.int32, (nb, blk), 0)
    past = n_iota < i
    k_idx = lax.broadcasted_iota(jnp.int32, (blk, blk), 0)
    q_idx = lax.broadcasted_iota(jnp.int32, (blk, blk), 1)
    causal = k_idx <= q_idx

    def stage1(hh, slot, s):
        s_ref[slot, hh] = s
        mx_ref[slot, hh] = jnp.max(s, axis=0, keepdims=True)

    def stage2(hh, slot, j):
        m_run = m_ref[hh]
        m_new = jnp.maximum(m_run, mx_ref[slot, hh])
        alpha = jnp.exp2(m_run - m_new)
        p = jnp.exp2(s_ref[slot, hh] - m_new).astype(BF16)
        m_ref[hh] = m_new
        acc_ref[hh] = alpha * acc_ref[hh] + jnp.dot(vt_ref[hh, j], p, preferred_element_type=F32)

    gate_scores = []
    for hh in range(heads):
        q = q_ref[:, head_cols(hh)]
        km = kmean_ref[:, head_cols(hh)]
        km_hi = km.astype(BF16)
        km_lo = (km - km_hi.astype(F32)).astype(BF16)
        sc = (lax.dot_general(km_hi, q, _TRANS_B, preferred_element_type=F32)
              + lax.dot_general(km_lo, q, _TRANS_B, preferred_element_type=F32))
        gate_scores.append(jnp.where(past, sc, NEG))

    for hh in range(heads):
        k_own = k_ref[pl.ds(pl.multiple_of(i * blk, blk), blk), head_cols(hh)]
        s_own = lax.dot_general(k_own, q_ref[:, head_cols(hh)], _TRANS_B, preferred_element_type=F32)
        stage1(hh, 0, jnp.where(causal, s_own, NEG))
        m_ref[hh] = jnp.full((1, blk), NEG, F32)
        acc_ref[hh] = jnp.zeros((dh + _SUM_ROWS, blk), F32)

    for hh in range(heads):
        work = gate_scores[hh]
        picked = jnp.zeros((nb, blk), jnp.bool_)
        for _ in range(topk):
            best = jnp.max(work, axis=0, keepdims=True)
            first = jnp.min(jnp.where(work == best, n_iota, nb), axis=0, keepdims=True)
            pick = n_iota == first
            picked = picked | pick
            work = jnp.where(pick, -jnp.inf, work)
        unselected = jnp.where(past & ~picked, 1.0, 0.0)
        unselected = jnp.concatenate([unselected, jnp.zeros((dh - nb, blk), F32)], axis=0)
        qa_ref[hh, :, 0:dh] = q_ref[:, head_cols(hh)]
        qa_ref[hh, :, dh:2 * dh] = unselected.T.astype(BF16)

    def visit(t, cur):
        j_cur = jnp.where(t == 0, i, t - 1)
        for hh in range(heads):
            stage1(hh, 1 - cur, scores(hh, t))
            stage2(hh, cur, j_cur)

    def visit_pair(u, carry):
        visit(2 * u, 0)
        visit(2 * u + 1, 1)
        return carry

    lax.fori_loop(0, i // 2, visit_pair, 0)

    j_last = jnp.where(i == 0, i, i - 1)

    @pl.when(i % 2 == 1)
    def _():
        visit(i - 1, 0)
        for hh in range(heads):
            stage2(hh, 1, j_last)

    @pl.when(i % 2 == 0)
    def _():
        for hh in range(heads):
            stage2(hh, 0, j_last)

    y = h_ref[...]
    for hh in range(heads):
        o = (acc_ref[hh, 0:dh, :] / acc_ref[hh, dh:dh + 1, :]).T
        og_ref[:, head_cols(hh)] = (o * _silu(gate_ref[:, head_cols(hh)])).astype(og_ref.dtype)
        if hh % 2 == 1:
            pair = slice((hh - 1) * dh, (hh + 1) * dh)
            y = y + jnp.dot(og_ref[:, pair], w_out_ref[pair, :], preferred_element_type=F32)
    if final_g_ref is not None:
        y = _rmsnorm(y, final_g_ref[...])
    o_ref[...] = y


def _moba_attn_kernel_plain(q_ref, k_ref, v_ref, gate_ref, h_ref, w_out_ref, o_ref, *scratch, **static):
    _moba_attn_kernel(q_ref, k_ref, v_ref, gate_ref, h_ref, w_out_ref, None, o_ref, *scratch, **static)


def _moba_attention(q, k, v, gate, h, w_out, final_g=None, *, batch, seq):
    n, width = q.shape
    d = h.shape[1]
    heads = ATT_HEADS
    assert heads % 2 == 0
    dh = width // heads
    blk = MOBA_BLOCK
    assert seq % blk == 0
    nb = seq // blk
    assert nb <= dh
    qspec = pl.BlockSpec((blk, width), lambda b, i: (b * nb + i, 0))
    hspec = pl.BlockSpec((blk, d), lambda b, i: (b * nb + i, 0))
    kspec = pl.BlockSpec((seq, width), lambda b, i: (b, 0))
    vspec = pl.BlockSpec((seq, width), lambda b, i: (b, 0), pipeline_mode=pl.Buffered(1))
    static = dict(blk=blk, nb=nb, topk=MOBA_TOPK, heads=heads, dh=dh)
    in_specs = [qspec, kspec, vspec, qspec, hspec, _const_spec(w_out.shape)]
    args = [q, k, v, gate, h, w_out]
    if final_g is None:
        body = functools.partial(_moba_attn_kernel_plain, **static)
    else:
        body = functools.partial(_moba_attn_kernel, **static)
        in_specs.append(_const_spec((1, d)))
        args.append(final_g)
    return pl.pallas_call(
        body,
        grid=(batch, nb),
        in_specs=in_specs,
        out_specs=hspec,
        out_shape=jax.ShapeDtypeStruct((n, d), F32),
        scratch_shapes=[
            pltpu.VMEM((nb, width), F32),
            pltpu.VMEM((heads, nb, dh + _SUM_ROWS, blk), BF16),
            pltpu.VMEM((nb, blk, dh), BF16),
            pltpu.VMEM((heads, blk, 2 * dh), BF16),
            pltpu.VMEM((2, heads, blk, blk), F32),
            pltpu.VMEM((2, heads, 1, blk), F32),
            pltpu.VMEM((heads, 1, blk), F32),
            pltpu.VMEM((heads, dh + _SUM_ROWS, blk), F32),
            pltpu.VMEM((blk, width), BF16),
        ],
        compiler_params=_params(2),
        name="moba_attention",
    )(*args)


_SUBLANES = 8
_LANES = 128
_SEG_PAD = 4


def _conformer_kernel(h_ref, g_ref, w_in_ref, cw_ref, cb_ref, lng_ref, lnb_ref, w_out_ref, o_ref,
                      y_ref, hist_ref, conv_ref, *, tiles_per_seq, kw):
    tm, c = o_ref.shape
    seg = tm // _SUBLANES
    pitch = seg + _SEG_PAD
    halo = kw - 1
    slabs = c // _LANES

    @pl.when(pl.program_id(0) % tiles_per_seq == 0)
    def _():
        hist_ref[...] = jnp.zeros_like(hist_ref)

    h = h_ref[...]
    xn = _rmsnorm(h, g_ref[...]).astype(BF16)
    a = jnp.dot(xn, w_in_ref[:, 0:c], preferred_element_type=F32)
    b = jnp.dot(xn, w_in_ref[:, c:2 * c], preferred_element_type=F32)
    y = a * _sigmoid(b)
    for lc in range(slabs):
        for s in range(_SUBLANES):
            y_ref[lc, s * pitch:s * pitch + seg, :] = y[s * seg:(s + 1) * seg, lc * _LANES:(lc + 1) * _LANES]

    first_segment = lax.broadcasted_iota(jnp.int32, (_SUBLANES, _LANES), 0) == 0
    conv_cols = []
    for lc in range(slabs):
        cols = slice(lc * _LANES, (lc + 1) * _LANES)
        rows = {u: y_ref[lc, pl.ds(u, _SUBLANES, stride=pitch), :] for u in range(seg)}
        for k in range(1, halo + 1):
            before = pltpu.roll(rows[seg - k], 1, 0)
            rows[-k] = jnp.where(first_segment, hist_ref[lc, seg - k:seg - k + 1, :], before)
        for u in range(seg):
            acc = jnp.broadcast_to(cb_ref[:, cols], (_SUBLANES, _LANES))
            for j in range(kw):
                acc = acc + cw_ref[j:j + 1, cols] * rows[u - halo + j]
            conv_ref[lc, pl.ds(u, _SUBLANES, stride=pitch), :] = acc
        hist_ref[lc] = y[tm - seg:tm, cols]
        conv_cols.append(jnp.concatenate(
            [conv_ref[lc, s * pitch:s * pitch + seg, :] for s in range(_SUBLANES)], axis=0))

    acc = jnp.concatenate(conv_cols, axis=1)
    mu = jnp.mean(acc, axis=-1, keepdims=True)
    cen = acc - mu
    var = jnp.mean(cen * cen, axis=-1, keepdims=True)
    y = cen * lax.rsqrt(var + EPS) * lng_ref[...] + lnb_ref[...]
    gate = jnp.dot(xn, w_in_ref[:, 2 * c:3 * c], preferred_element_type=F32)
    z = (_silu(y) * _silu(gate)).astype(BF16)
    o_ref[...] = h + jnp.dot(z, w_out_ref[...], preferred_element_type=F32)


def _conformer_layer(h, g, w_in, conv_w, conv_b, ln_g, ln_b, w_out, *, seq, tm=256):
    n, d = h.shape
    c = w_out.shape[0]
    kw = conv_w.shape[0]
    seg = tm // _SUBLANES
    assert seq % tm == 0 and tm % (_SUBLANES * _SUBLANES) == 0 and c % _LANES == 0 and kw - 1 <= seg
    seg_rows = _SUBLANES * (seg + _SEG_PAD)
    return pl.pallas_call(
        functools.partial(_conformer_kernel, tiles_per_seq=seq // tm, kw=kw),
        grid=(n // tm,),
        in_specs=[
            pl.BlockSpec((tm, d), lambda i: (i, 0)),
            _const_spec((1, d)),
            _const_spec(w_in.shape),
            _const_spec(conv_w.shape),
            _const_spec((1, c)),
            _const_spec((1, c)),
            _const_spec((1, c)),
            _const_spec(w_out.shape),
        ],
        out_specs=pl.BlockSpec((tm, d), lambda i: (i, 0)),
        out_shape=jax.ShapeDtypeStruct((n, d), F32),
        scratch_shapes=[
            pltpu.VMEM((c // _LANES, seg_rows, _LANES), F32),
            pltpu.VMEM((c // _LANES, seg, _LANES), F32),
            pltpu.VMEM((c // _LANES, seg_rows, _LANES), F32),
        ],
        compiler_params=_params(1),
        name="conformer_layer",
    )(h, g, w_in, conv_w, conv_b, ln_g, ln_b, w_out)


_LRU_HIST = 8


def _rglru_kernel(h_ref, g_ref, w_in_ref, cw_ref, cb_ref, w_gates_ref, b_rg_ref, b_ig_ref, lam_ref,
                  w_out_ref, o_ref, xbuf_ref, a_ref, b_ref, hs_ref, state_ref, *, tiles_per_seq, kw, heads):
    tm = o_ref.shape[0]
    w = w_out_ref.shape[0]
    hd = w // heads
    hist = _LRU_HIST
    seg = tm // _SUBLANES
    pitch = seg + _SEG_PAD

    @pl.when(pl.program_id(0) % tiles_per_seq == 0)
    def _():
        xbuf_ref[0:hist, :] = jnp.zeros((hist, w), F32)
        state_ref[...] = jnp.zeros_like(state_ref)

    h = h_ref[...]
    xn = _rmsnorm(h, g_ref[...]).astype(BF16)
    xbuf_ref[hist:hist + tm, :] = jnp.dot(xn, w_in_ref[:, 0:w], preferred_element_type=F32)

    base = hist - (kw - 1)
    xc = cb_ref[...] + cw_ref[0:1, :] * xbuf_ref[base:base + tm, :]
    for j in range(1, kw):
        xc = xc + cw_ref[j:j + 1, :] * xbuf_ref[base + j:base + j + tm, :]
    xbuf_ref[0:hist, :] = xbuf_ref[tm:tm + hist, :]

    neg_lam = -lam_ref[...]
    c_softplus = LRU_C * (jnp.maximum(neg_lam, 0.0) + jnp.log1p(jnp.exp(-jnp.abs(neg_lam))))
    xc_bf = xc.astype(BF16)
    for hh in range(heads):
        cols = slice(hh * hd, (hh + 1) * hd)
        gates = jnp.dot(xc_bf[:, cols], w_gates_ref[hh], preferred_element_type=F32)
        r = _sigmoid(gates[:, 0:hd] + b_rg_ref[:, cols])
        ig = _sigmoid(gates[:, hd:2 * hd] + b_ig_ref[:, cols])
        log_a = -(r * c_softplus[:, cols])
        a = jnp.exp(log_a)
        one_minus_a2 = -jnp.tanh(log_a) * (a * a + 1.0)
        b = jnp.sqrt(one_minus_a2) * (ig * xc[:, cols])
        for s in range(_SUBLANES):
            a_ref[hh, s * pitch:s * pitch + seg, :] = a[s * seg:(s + 1) * seg, :]
            b_ref[hh, s * pitch:s * pitch + seg, :] = b[s * seg:(s + 1) * seg, :]

    def step_rows(ref, hh, u):
        return ref[hh, pl.ds(u, _SUBLANES, stride=pitch), :]

    hs_cols = []
    for hh in range(heads):
        cols = slice(hh * hd, (hh + 1) * hd)
        prod = jnp.ones((_SUBLANES, hd), F32)
        total = jnp.zeros((_SUBLANES, hd), F32)
        for u in range(seg):
            au = step_rows(a_ref, hh, u)
            total = au * total + step_rows(b_ref, hh, u)
            prod = au * prod
        cur = state_ref[:, cols]
        starts = []
        for s in range(_SUBLANES):
            starts.append(cur)
            cur = prod[s:s + 1, :] * cur + total[s:s + 1, :]
        state_ref[:, cols] = cur
        hv = jnp.concatenate(starts, axis=0)
        for u in range(seg):
            hv = step_rows(a_ref, hh, u) * hv + step_rows(b_ref, hh, u)
            hs_ref[hh, pl.ds(u, _SUBLANES, stride=pitch), :] = hv
        hs_cols.append(jnp.concatenate(
            [hs_ref[hh, s * pitch:s * pitch + seg, :] for s in range(_SUBLANES)], axis=0))

    gate = jnp.dot(xn, w_in_ref[:, w:2 * w], preferred_element_type=F32)
    y = (jnp.concatenate(hs_cols, axis=1) * _silu(gate)).astype(BF16)
    o_ref[...] = h + jnp.dot(y, w_out_ref[...], preferred_element_type=F32)


def _rglru_layer(h, g, w_in, conv_w, conv_b, w_gates, b_rg, b_ig, lam, w_out, *, seq, tm=256):
    n, d = h.shape
    w = w_out.shape[0]
    kw = conv_w.shape[0]
    assert kw - 1 <= _LRU_HIST <= tm and seq % tm == 0 and tm % (_SUBLANES * _SUBLANES) == 0
    hd = w // LRU_HEADS
    assert hd == _LANES
    seg_rows = _SUBLANES * (tm // _SUBLANES + _SEG_PAD)
    return pl.pallas_call(
        functools.partial(_rglru_kernel, tiles_per_seq=seq // tm, kw=kw, heads=LRU_HEADS),
        grid=(n // tm,),
        in_specs=[
            pl.BlockSpec((tm, d), lambda i: (i, 0)),
            _const_spec((1, d)),
            _const_spec(w_in.shape),
            _const_spec(conv_w.shape),
            _const_spec((1, w)),
            _const_spec(w_gates.shape),
            _const_spec((1, w)),
            _const_spec((1, w)),
            _const_spec((1, w)),
            _const_spec(w_out.shape),
        ],
        out_specs=pl.BlockSpec((tm, d), lambda i: (i, 0)),
        out_shape=jax.ShapeDtypeStruct((n, d), F32),
        scratch_shapes=[
            pltpu.VMEM((_LRU_HIST + tm, w), F32),
            pltpu.VMEM((LRU_HEADS, seg_rows, hd), F32),
            pltpu.VMEM((LRU_HEADS, seg_rows, hd), F32),
            pltpu.VMEM((LRU_HEADS, seg_rows, hd), F32),
            pltpu.VMEM((1, w), F32),
        ],
        compiler_params=_params(1),
        name="rglru_layer",
    )(h, g, w_in, conv_w, conv_b, w_gates, b_rg, b_ig, lam, w_out)


def _moba_layer(h, norm_g, w_in, w_out, *, batch, seq, final_g=None):
    head_dim = w_out.shape[0] // ATT_HEADS
    q, k, v, gate = _rms_qkvg(h, norm_g, w_in.astype(BF16), head_dim=head_dim)
    return _moba_attention(q, k, v, gate, h, w_out.astype(BF16), final_g, batch=batch, seq=seq)


def _row(p):
    return p.reshape(1, -1)


def kernel(x, l0_norm_g, l0_w_in, l0_w_out, l1_norm_g, l1_w_in, l1_conv_w, l1_conv_b, l1_ln_g, l1_ln_b, l1_w_out, l2_norm_g, l2_w_in, l2_conv_w, l2_conv_b, l2_w_rg, l2_b_rg, l2_w_ig, l2_b_ig, l2_lam, l2_w_out, l3_norm_g, l3_w_in, l3_w_out, final_g):
    batch, seq, d = x.shape
    h = x.reshape(batch * seq, d)

    h = _moba_layer(h, _row(l0_norm_g), l0_w_in, l0_w_out, batch=batch, seq=seq)

    h = _conformer_layer(h, _row(l1_norm_g), l1_w_in.astype(BF16), l1_conv_w, _row(l1_conv_b),
                         _row(l1_ln_g), _row(l1_ln_b), l1_w_out.astype(BF16), seq=seq)

    w_gates = jnp.concatenate([l2_w_rg, l2_w_ig], axis=-1).astype(BF16)
    h = _rglru_layer(h, _row(l2_norm_g), l2_w_in.astype(BF16), l2_conv_w, _row(l2_conv_b), w_gates,
                     _row(l2_b_rg), _row(l2_b_ig), _row(l2_lam), l2_w_out.astype(BF16), seq=seq)

    out = _moba_layer(h, _row(l3_norm_g), l3_w_in, l3_w_out, batch=batch, seq=seq, final_g=_row(final_g))
    return out.reshape(batch, seq, d)
```

```python
import functools

import jax
import jax.numpy as jnp
from jax import lax
from jax.experimental import pallas as pl
from jax.experimental.pallas import tpu as pltpu

F32 = jnp.float32
BF16 = jnp.bfloat16

EPS = 1e-6
NEG = -1e30

ATT_HEADS = 8
MOBA_BLOCK = 256
MOBA_TOPK = 3
LRU_HEADS = 10
LRU_C = 8.0

VMEM_LIMIT_BYTES = 56 * 1024 * 1024
LOG2_E = 1.4426950408889634


def _const_spec(shape):
    zeros = (0,) * len(shape)
    return pl.BlockSpec(shape, lambda *_: zeros, pipeline_mode=pl.Buffered(1))


def _params(n_axes):
    return pltpu.CompilerParams(
        dimension_semantics=("arbitrary",) * n_axes,
        vmem_limit_bytes=VMEM_LIMIT_BYTES,
    )


def _rmsnorm(x, g):
    return x * lax.rsqrt(jnp.mean(x * x, axis=-1, keepdims=True) + EPS) * g


def _sigmoid(x):
    return 0.5 * jnp.tanh(0.5 * x) + 0.5


def _silu(x):
    u = 0.5 * x
    return u * jnp.tanh(u) + u


def _rms_qkvg_kernel(x_ref, g_ref, w_ref, q_ref, k_ref, v_ref, gate_ref, *, q_scale):
    xn = _rmsnorm(x_ref[...], g_ref[...]).astype(BF16)
    width = q_ref.shape[1]
    for c, ref in enumerate((q_ref, k_ref, v_ref, gate_ref)):
        acc = jnp.dot(xn, w_ref[:, c * width:(c + 1) * width], preferred_element_type=F32)
        if c == 0:
            acc = acc * q_scale
        ref[...] = acc.astype(ref.dtype)


def _rms_qkvg(h, g, w_in, *, head_dim, tm=512):
    n, d = h.shape
    width = w_in.shape[1] // 4
    out_bf = jax.ShapeDtypeStruct((n, width), BF16)
    row_spec = pl.BlockSpec((tm, width), lambda i: (i, 0))
    return pl.pallas_call(
        functools.partial(_rms_qkvg_kernel, q_scale=head_dim ** -0.5 * LOG2_E),
        grid=(n // tm,),
        in_specs=[
            pl.BlockSpec((tm, d), lambda i: (i, 0)),
            _const_spec((1, d)),
            _const_spec(w_in.shape),
        ],
        out_specs=[row_spec, row_spec, row_spec, row_spec],
        out_shape=[out_bf, out_bf, out_bf, jax.ShapeDtypeStruct((n, width), F32)],
        compiler_params=_params(1),
        name="rms_qkvg",
    )(h, g, w_in)


_TRANS_B = (((1,), (1,)), ((), ()))
_TRANS_A = (((0,), (0,)), ((), ()))


_SUM_ROWS = 16


def _moba_attn_kernel(q_ref, k_ref, v_ref, gate_ref, h_ref, w_out_ref, final_g_ref, o_ref, kmean_ref, vt_ref,
                      bias_ref, qa_ref, s_ref, mx_ref, m_ref, acc_ref, og_ref, *, blk, nb, topk, heads, dh):
    i = pl.program_id(1)

    def head_cols(hh):
        return slice(hh * dh, (hh + 1) * dh)

    @pl.when(i == 0)
    def _():
        lane = lax.broadcasted_iota(jnp.int32, (blk, dh), 1)

        def prep_block(j, carry):
            rows = pl.ds(pl.multiple_of(j * blk, blk), blk)
            kmean_ref[pl.ds(j, 1), :] = jnp.mean(k_ref[rows, :].astype(F32), axis=0, keepdims=True)
            bias_ref[j] = jnp.where(lane == j, NEG, 0.0).astype(BF16)
            for hh in range(heads):
                vt_ref[hh, j, 0:dh, :] = v_ref[rows, head_cols(hh)].astype(F32).T.astype(BF16)
                vt_ref[hh, j, dh:dh + _SUM_ROWS, :] = jnp.ones((_SUM_ROWS, blk), BF16)
            return carry

        lax.fori_loop(0, nb, prep_block, 0)

    def scores(hh, j):
        kj = k_ref[pl.ds(pl.multiple_of(j * blk, blk), blk), head_cols(hh)]
        k_aug = jnp.concatenate([kj, bias_ref[j]], axis=1)
        return lax.dot_general(k_aug, qa_ref[hh], _TRANS_B, preferred_element_type=F32)

    n_iota = lax.broadcasted_iota(jnp.int32, (nb, blk), 0)
    past = n_iota < i
    k_idx = lax.broadcasted_iota(jnp.int32, (blk, blk), 0)
    q_idx = lax.broadcasted_iota(jnp.int32, (blk, blk), 1)
    causal = k_idx <= q_idx

    def stage1(hh, slot, s):
        s_ref[slot, hh] = s
        mx_ref[slot, hh] = jnp.max(s, axis=0, keepdims=True)

    def stage2(hh, slot, j):
        m_run = m_ref[hh]
        m_new = jnp.maximum(m_run, mx_ref[slot, hh])
        alpha = jnp.exp2(m_run - m_new)
        p = jnp.exp2(s_ref[slot, hh] - m_new).astype(BF16)
        m_ref[hh] = m_new
        acc_ref[hh] = alpha * acc_ref[hh] + jnp.dot(vt_ref[hh, j], p, preferred_element_type=F32)

    gate_scores = []
    for hh in range(heads):
        q = q_ref[:, head_cols(hh)]
        km = kmean_ref[:, head_cols(hh)]
        km_hi = km.astype(BF16)
        km_lo = (km - km_hi.astype(F32)).astype(BF16)
        sc = (lax.dot_general(km_hi, q, _TRANS_B, preferred_element_type=F32)
              + lax.dot_general(km_lo, q, _TRANS_B, preferred_element_type=F32))
        gate_scores.append(jnp.where(past, sc, NEG))

    for hh in range(heads):
        k_own = k_ref[pl.ds(pl.multiple_of(i * blk, blk), blk), head_cols(hh)]
        s_own = lax.dot_general(k_own, q_ref[:, head_cols(hh)], _TRANS_B, preferred_element_type=F32)
        stage1(hh, 0, jnp.where(causal, s_own, NEG))
        m_ref[hh] = jnp.full((1, blk), NEG, F32)
        acc_ref[hh] = jnp.zeros((dh + _SUM_ROWS, blk), F32)

    for hh in range(heads):
        work = gate_scores[hh]
        picked = jnp.zeros((nb, blk), jnp.bool_)
        for _ in range(topk):
            best = jnp.max(work, axis=0, keepdims=True)
            first = jnp.min(jnp.where(work == best, n_iota, nb), axis=0, keepdims=True)
            pick = n_iota == first
            picked = picked | pick
            work = jnp.where(pick, -jnp.inf, work)
        unselected = jnp.where(past & ~picked, 1.0, 0.0)
        unselected = jnp.concatenate([unselected, jnp.zeros((dh - nb, blk), F32)], axis=0)
        qa_ref[hh, :, 0:dh] = q_ref[:, head_cols(hh)]
        qa_ref[hh, :, dh:2 * dh] = unselected.T.astype(BF16)

    def visit(t, cur):
        j_cur = jnp.where(t == 0, i, t - 1)
        for hh in range(heads):
            stage1(hh, 1 - cur, scores(hh, t))
            stage2(hh, cur, j_cur)

    def visit_pair(u, carry):
        visit(2 * u, 0)
        visit(2 * u + 1, 1)
        return carry

    lax.fori_loop(0, i // 2, visit_pair, 0)

    j_last = jnp.where(i == 0, i, i - 1)

    @pl.when(i % 2 == 1)
    def _():
        visit(i - 1, 0)
        for hh in range(heads):
            stage2(hh, 1, j_last)

    @pl.when(i % 2 == 0)
    def _():
        for hh in range(heads):
            stage2(hh, 0, j_last)

    y = h_ref[...]
    for hh in range(heads):
        o = (acc_ref[hh, 0:dh, :] / acc_ref[hh, dh:dh + 1, :]).T
        og_ref[:, head_cols(hh)] = (o * _silu(gate_ref[:, head_cols(hh)])).astype(og_ref.dtype)
        if hh % 2 == 1:
            pair = slice((hh - 1) * dh, (hh + 1) * dh)
            y = y + jnp.dot(og_ref[:, pair], w_out_ref[pair, :], preferred_element_type=F32)
    if final_g_ref is not None:
        y = _rmsnorm(y, final_g_ref[...])
    o_ref[...] = y


def _moba_attn_kernel_plain(q_ref, k_ref, v_ref, gate_ref, h_ref, w_out_ref, o_ref, *scratch, **static):
    _moba_attn_kernel(q_ref, k_ref, v_ref, gate_ref, h_ref, w_out_ref, None, o_ref, *scratch, **static)


def _moba_attention(q, k, v, gate, h, w_out, final_g=None, *, batch, seq):
    n, width = q.shape
    d = h.shape[1]
    heads = ATT_HEADS
    assert heads % 2 == 0
    dh = width // heads
    blk = MOBA_BLOCK
    assert seq % blk == 0
    nb = seq // blk
    assert nb <= dh
    qspec = pl.BlockSpec((blk, width), lambda b, i: (b * nb + i, 0))
    hspec = pl.BlockSpec((blk, d), lambda b, i: (b * nb + i, 0))
    kspec = pl.BlockSpec((seq, width), lambda b, i: (b, 0))
    vspec = pl.BlockSpec((seq, width), lambda b, i: (b, 0), pipeline_mode=pl.Buffered(1))
    static = dict(blk=blk, nb=nb, topk=MOBA_TOPK, heads=heads, dh=dh)
    in_specs = [qspec, kspec, vspec, qspec, hspec, _const_spec(w_out.shape)]
    args = [q, k, v, gate, h, w_out]
    if final_g is None:
        body = functools.partial(_moba_attn_kernel_plain, **static)
    else:
        body = functools.partial(_moba_attn_kernel, **static)
        in_specs.append(_const_spec((1, d)))
        args.append(final_g)
    return pl.pallas_call(
        body,
        grid=(batch, nb),
        in_specs=in_specs,
        out_specs=hspec,
        out_shape=jax.ShapeDtypeStruct((n, d), F32),
        scratch_shapes=[
            pltpu.VMEM((nb, width), F32),
            pltpu.VMEM((heads, nb, dh + _SUM_ROWS, blk), BF16),
            pltpu.VMEM((nb, blk, dh), BF16),
            pltpu.VMEM((heads, blk, 2 * dh), BF16),
            pltpu.VMEM((2, heads, blk, blk), F32),
            pltpu.VMEM((2, heads, 1, blk), F32),
            pltpu.VMEM((heads, 1, blk), F32),
            pltpu.VMEM((heads, dh + _SUM_ROWS, blk), F32),
            pltpu.VMEM((blk, width), BF16),
        ],
        compiler_params=_params(2),
        name="moba_attention",
    )(*args)


_SUBLANES = 8
_LANES = 128
_SEG_PAD = 4


def _conformer_kernel(h_ref, g_ref, w_in_ref, cw_ref, cb_ref, lng_ref, lnb_ref, w_out_ref, o_ref,
                      y_ref, hist_ref, conv_ref, *, tiles_per_seq, kw):
    tm, c = o_ref.shape
    seg = tm // _SUBLANES
    pitch = seg + _SEG_PAD
    halo = kw - 1
    slabs = c // _LANES

    @pl.when(pl.program_id(0) % tiles_per_seq == 0)
    def _():
        hist_ref[...] = jnp.zeros_like(hist_ref)

    h = h_ref[...]
    xn = _rmsnorm(h, g_ref[...]).astype(BF16)
    a = jnp.dot(xn, w_in_ref[:, 0:c], preferred_element_type=F32)
    b = jnp.dot(xn, w_in_ref[:, c:2 * c], preferred_element_type=F32)
    y = a * _sigmoid(b)
    for lc in range(slabs):
        for s in range(_SUBLANES):
            y_ref[lc, s * pitch:s * pitch + seg, :] = y[s * seg:(s + 1) * seg, lc * _LANES:(lc + 1) * _LANES]

    first_segment = lax.broadcasted_iota(jnp.int32, (_SUBLANES, _LANES), 0) == 0
    conv_cols = []
    for lc in range(slabs):
        cols = slice(lc * _LANES, (lc + 1) * _LANES)
        rows = {u: y_ref[lc, pl.ds(u, _SUBLANES, stride=pitch), :] for u in range(seg)}
        for k in range(1, halo + 1):
            before = pltpu.roll(rows[seg - k], 1, 0)
            rows[-k] = jnp.where(first_segment, hist_ref[lc, seg - k:seg - k + 1, :], before)
        for u in range(seg):
            acc = jnp.broadcast_to(cb_ref[:, cols], (_SUBLANES, _LANES))
            for j in range(kw):
                acc = acc + cw_ref[j:j + 1, cols] * rows[u - halo + j]
            conv_ref[lc, pl.ds(u, _SUBLANES, stride=pitch), :] = acc
        hist_ref[lc] = y[tm - seg:tm, cols]
        conv_cols.append(jnp.concatenate(
            [conv_ref[lc, s * pitch:s * pitch + seg, :] for s in range(_SUBLANES)], axis=0))

    acc = jnp.concatenate(conv_cols, axis=1)
    mu = jnp.mean(acc, axis=-1, keepdims=True)
    cen = acc - mu
    var = jnp.mean(cen * cen, axis=-1, keepdims=True)
    y = cen * lax.rsqrt(var + EPS) * lng_ref[...] + lnb_ref[...]
    gate = jnp.dot(xn, w_in_ref[:, 2 * c:3 * c], preferred_element_type=F32)
    z = (_silu(y) * _silu(gate)).astype(BF16)
    o_ref[...] = h + jnp.dot(z, w_out_ref[...], preferred_element_type=F32)


def _conformer_layer(h, g, w_in, conv_w, conv_b, ln_g, ln_b, w_out, *, seq, tm=256):
    n, d = h.shape
    c = w_out.shape[0]
    kw = conv_w.shape[0]
    seg = tm // _SUBLANES
    assert seq % tm == 0 and tm % (_SUBLANES * _SUBLANES) == 0 and c % _LANES == 0 and kw - 1 <= seg
    seg_rows = _SUBLANES * (seg + _SEG_PAD)
    return pl.pallas_call(
        functools.partial(_conformer_kernel, tiles_per_seq=seq // tm, kw=kw),
        grid=(n // tm,),
        in_specs=[
            pl.BlockSpec((tm, d), lambda i: (i, 0)),
            _const_spec((1, d)),
            _const_spec(w_in.shape),
            _const_spec(conv_w.shape),
            _const_spec((1, c)),
            _const_spec((1, c)),
            _const_spec((1, c)),
            _const_spec(w_out.shape),
        ],
        out_specs=pl.BlockSpec((tm, d), lambda i: (i, 0)),
        out_shape=jax.ShapeDtypeStruct((n, d), F32),
        scratch_shapes=[
            pltpu.VMEM((c // _LANES, seg_rows, _LANES), F32),
            pltpu.VMEM((c // _LANES, seg, _LANES), F32),
            pltpu.VMEM((c // _LANES, seg_rows, _LANES), F32),
        ],
        compiler_params=_params(1),
        name="conformer_layer",
    )(h, g, w_in, conv_w, conv_b, ln_g, ln_b, w_out)


def _rglru_kernel(h_ref, g_ref, w_in_ref, cw_ref, cb_ref, w_gates_ref, b_rg_ref, b_ig_ref, lam_ref,
                  w_out_ref, o_ref, hseg_ref, oseg_ref, hist_ref, state_ref, *, tiles_per_seq, kw, heads):
    tm, d = o_ref.shape
    w = w_out_ref.shape[0]
    hd = w // heads
    seg = tm // _SUBLANES
    pitch = seg + _SEG_PAD
    halo = kw - 1
    slabs = d // _LANES

    @pl.when(pl.program_id(0) % tiles_per_seq == 0)
    def _():
        hist_ref[...] = jnp.zeros_like(hist_ref)
        state_ref[...] = jnp.zeros_like(state_ref)

    def step(x, u):
        return x[u * _SUBLANES:(u + 1) * _SUBLANES, :]

    for lc in range(slabs):
        for s in range(_SUBLANES):
            hseg_ref[lc, s * pitch:s * pitch + seg, :] = h_ref[s * seg:(s + 1) * seg, lc * _LANES:(lc + 1) * _LANES]
    h = jnp.concatenate(
        [jnp.concatenate([hseg_ref[lc, pl.ds(u, _SUBLANES, stride=pitch), :] for u in range(seg)], axis=0)
         for lc in range(slabs)], axis=1)

    xn = _rmsnorm(h, g_ref[...]).astype(BF16)
    xb = jnp.dot(xn, w_in_ref[:, 0:w], preferred_element_type=F32)

    first_segment = lax.broadcasted_iota(jnp.int32, (_SUBLANES, w), 0) == 0
    rows = {u: step(xb, u) for u in range(seg)}
    for k in range(1, halo + 1):
        rows[-k] = jnp.where(first_segment, hist_ref[k - 1:k, :], pltpu.roll(rows[seg - k], 1, 0))
    for k in range(1, halo + 1):
        hist_ref[k - 1:k, :] = rows[seg - k][_SUBLANES - 1:_SUBLANES, :]
    conv_steps = []
    for u in range(seg):
        acc = cb_ref[...] + cw_ref[0:1, :] * rows[u - halo]
        for j in range(1, kw):
            acc = acc + cw_ref[j:j + 1, :] * rows[u - halo + j]
        conv_steps.append(acc)
    xc = jnp.concatenate(conv_steps, axis=0)

    neg_lam = -lam_ref[...]
    c_softplus = LRU_C * (jnp.maximum(neg_lam, 0.0) + jnp.log1p(jnp.exp(-jnp.abs(neg_lam))))
    neg_c_log2e = -(c_softplus * LOG2_E)
    xc_bf = xc.astype(BF16)
    hs_cols = []
    gate_cols = []
    for hh in range(heads):
        cols = slice(hh * hd, (hh + 1) * hd)
        gates = jnp.dot(xc_bf[:, cols], w_gates_ref[hh], preferred_element_type=F32)
        r = _sigmoid(gates[:, 0:hd] + b_rg_ref[:, cols])
        ig = _sigmoid(gates[:, hd:2 * hd] + b_ig_ref[:, cols])
        a = jnp.exp2(r * neg_c_log2e[:, cols])
        one_minus_a2 = jnp.tanh(r * c_softplus[:, cols]) * (a * a + 1.0)
        b = jnp.sqrt(one_minus_a2) * (ig * xc[:, cols])

        prod = jnp.ones((_SUBLANES, hd), F32)
        total = jnp.zeros((_SUBLANES, hd), F32)
        for u in range(seg):
            au = step(a, u)
            total = au * total + step(b, u)
            prod = au * prod
        cur = state_ref[:, cols]
        starts = []
        for s in range(_SUBLANES):
            starts.append(cur)
            cur = prod[s:s + 1, :] * cur + total[s:s + 1, :]
        state_ref[:, cols] = cur
        hv = jnp.concatenate(starts, axis=0)
        states = []
        for u in range(seg):
            hv = step(a, u) * hv + step(b, u)
            states.append(hv)
        hs_cols.append(jnp.concatenate(states, axis=0))
        if hh % 2 == 1:
            lo = (hh - 1) * hd
            gate_cols.append(jnp.dot(xn, w_in_ref[:, w + lo:w + lo + 2 * hd], preferred_element_type=F32))

    y = (jnp.concatenate(hs_cols, axis=1) * _silu(jnp.concatenate(gate_cols, axis=1))).astype(BF16)
    out = h + jnp.dot(y, w_out_ref[...], preferred_element_type=F32)

    for lc in range(slabs):
        cols = slice(lc * _LANES, (lc + 1) * _LANES)
        for u in range(seg):
            oseg_ref[lc, pl.ds(u, _SUBLANES, stride=pitch), :] = step(out[:, cols], u)
        for s in range(_SUBLANES):
            o_ref[s * seg:(s + 1) * seg, cols] = oseg_ref[lc, s * pitch:s * pitch + seg, :]


def _rglru_layer(h, g, w_in, conv_w, conv_b, w_gates, b_rg, b_ig, lam, w_out, *, seq, tm=256):
    n, d = h.shape
    w = w_out.shape[0]
    kw = conv_w.shape[0]
    seg = tm // _SUBLANES
    assert seq % tm == 0 and tm % (_SUBLANES * _SUBLANES) == 0 and d % _LANES == 0 and kw - 1 <= _SUBLANES
    assert LRU_HEADS % 2 == 0
    seg_rows = _SUBLANES * (seg + _SEG_PAD)
    return pl.pallas_call(
        functools.partial(_rglru_kernel, tiles_per_seq=seq // tm, kw=kw, heads=LRU_HEADS),
        grid=(n // tm,),
        in_specs=[
            pl.BlockSpec((tm, d), lambda i: (i, 0)),
            _const_spec((1, d)),
            _const_spec(w_in.shape),
            _const_spec(conv_w.shape),
            _const_spec((1, w)),
            _const_spec(w_gates.shape),
            _const_spec((1, w)),
            _const_spec((1, w)),
            _const_spec((1, w)),
            _const_spec(w_out.shape),
        ],
        out_specs=pl.BlockSpec((tm, d), lambda i: (i, 0)),
        out_shape=jax.ShapeDtypeStruct((n, d), F32),
        scratch_shapes=[
            pltpu.VMEM((d // _LANES, seg_rows, _LANES), F32),
            pltpu.VMEM((d // _LANES, seg_rows, _LANES), F32),
            pltpu.VMEM((_SUBLANES, w), F32),
            pltpu.VMEM((1, w), F32),
        ],
        compiler_params=_params(1),
        name="rglru_layer",
    )(h, g, w_in, conv_w, conv_b, w_gates, b_rg, b_ig, lam, w_out)


def _moba_layer(h, norm_g, w_in, w_out, *, batch, seq, final_g=None):
    head_dim = w_out.shape[0] // ATT_HEADS
    q, k, v, gate = _rms_qkvg(h, norm_g, w_in.astype(BF16), head_dim=head_dim)
    return _moba_attention(q, k, v, gate, h, w_out.astype(BF16), final_g, batch=batch, seq=seq)


def _row(p):
    return p.reshape(1, -1)


def kernel(x, l0_norm_g, l0_w_in, l0_w_out, l1_norm_g, l1_w_in, l1_conv_w, l1_conv_b, l1_ln_g, l1_ln_b, l1_w_out, l2_norm_g, l2_w_in, l2_conv_w, l2_conv_b, l2_w_rg, l2_b_rg, l2_w_ig, l2_b_ig, l2_lam, l2_w_out, l3_norm_g, l3_w_in, l3_w_out, final_g):
    batch, seq, d = x.shape
    h = x.reshape(batch * seq, d)

    h = _moba_layer(h, _row(l0_norm_g), l0_w_in, l0_w_out, batch=batch, seq=seq)

    h = _conformer_layer(h, _row(l1_norm_g), l1_w_in.astype(BF16), l1_conv_w, _row(l1_conv_b),
                         _row(l1_ln_g), _row(l1_ln_b), l1_w_out.astype(BF16), seq=seq)

    w_gates = jnp.concatenate([l2_w_rg, l2_w_ig], axis=-1).astype(BF16)
    h = _rglru_layer(h, _row(l2_norm_g), l2_w_in.astype(BF16), l2_conv_w, _row(l2_conv_b), w_gates,
                     _row(l2_b_rg), _row(l2_b_ig), _row(l2_lam), l2_w_out.astype(BF16), seq=seq)

    out = _moba_layer(h, _row(l3_norm_g), l3_w_in, l3_w_out, batch=batch, seq=seq, final_g=_row(final_g))
    return out.reshape(batch, seq, d)
```

```python
import functools

import jax
import jax.numpy as jnp
from jax import lax
from jax.experimental import pallas as pl
from jax.experimental.pallas import tpu as pltpu

F32 = jnp.float32
BF16 = jnp.bfloat16

EPS = 1e-6
NEG = -1e30

ATT_HEADS = 8
MOBA_BLOCK = 256
MOBA_TOPK = 3
LRU_HEADS = 10
LRU_C = 8.0

VMEM_LIMIT_BYTES = 56 * 1024 * 1024
LOG2_E = 1.4426950408889634


def _const_spec(shape):
    zeros = (0,) * len(shape)
    return pl.BlockSpec(shape, lambda *_: zeros, pipeline_mode=pl.Buffered(1))


def _params(n_axes):
    return pltpu.CompilerParams(
        dimension_semantics=("arbitrary",) * n_axes,
        vmem_limit_bytes=VMEM_LIMIT_BYTES,
    )


def _rmsnorm(x, g):
    return x * lax.rsqrt(jnp.mean(x * x, axis=-1, keepdims=True) + EPS) * g


def _sigmoid(x):
    return 0.5 * jnp.tanh(0.5 * x) + 0.5


def _silu(x):
    u = 0.5 * x
    return u * jnp.tanh(u) + u


def _rms_qkvg_kernel(x_ref, g_ref, w_ref, q_ref, k_ref, v_ref, gate_ref, *, q_scale):
    xn = _rmsnorm(x_ref[...], g_ref[...]).astype(BF16)
    width = q_ref.shape[1]
    for c, ref in enumerate((q_ref, k_ref, v_ref, gate_ref)):
        acc = jnp.dot(xn, w_ref[:, c * width:(c + 1) * width], preferred_element_type=F32)
        if c == 0:
            acc = acc * q_scale
        ref[...] = acc.astype(ref.dtype)


def _rms_qkvg(h, g, w_in, *, head_dim, tm=1024):
    n, d = h.shape
    width = w_in.shape[1] // 4
    out_bf = jax.ShapeDtypeStruct((n, width), BF16)
    row_spec = pl.BlockSpec((tm, width), lambda i: (i, 0))
    return pl.pallas_call(
        functools.partial(_rms_qkvg_kernel, q_scale=head_dim ** -0.5 * LOG2_E),
        grid=(n // tm,),
        in_specs=[
            pl.BlockSpec((tm, d), lambda i: (i, 0)),
            _const_spec((1, d)),
            _const_spec(w_in.shape),
        ],
        out_specs=[row_spec, row_spec, row_spec, row_spec],
        out_shape=[out_bf, out_bf, out_bf, jax.ShapeDtypeStruct((n, width), F32)],
        compiler_params=_params(1),
        name="rms_qkvg",
    )(h, g, w_in)


_TRANS_B = (((1,), (1,)), ((), ()))
_TRANS_A = (((0,), (0,)), ((), ()))


_SUM_ROWS = 16


def _moba_attn_kernel(q_ref, k_ref, v_ref, gate_ref, h_ref, w_out_ref, final_g_ref, o_ref, kmean_ref, vt_ref,
                      bias_ref, qa_ref, s_ref, mx_ref, m_ref, acc_ref, og_ref, *, blk, nb, topk, heads, dh):
    i = pl.program_id(1)

    def head_cols(hh):
        return slice(hh * dh, (hh + 1) * dh)

    @pl.when(i == 0)
    def _():
        lane = lax.broadcasted_iota(jnp.int32, (blk, dh), 1)

        def prep_block(j, carry):
            rows = pl.ds(pl.multiple_of(j * blk, blk), blk)
            kmean_ref[pl.ds(j, 1), :] = jnp.mean(k_ref[rows, :].astype(F32), axis=0, keepdims=True)
            bias_ref[j] = jnp.where(lane == j, NEG, 0.0).astype(BF16)
            for hh in range(heads):
                vt_ref[hh, j, 0:dh, :] = v_ref[rows, head_cols(hh)].astype(F32).T.astype(BF16)
                vt_ref[hh, j, dh:dh + _SUM_ROWS, :] = jnp.ones((_SUM_ROWS, blk), BF16)
            return carry

        lax.fori_loop(0, nb, prep_block, 0)

    def scores(hh, j):
        kj = k_ref[pl.ds(pl.multiple_of(j * blk, blk), blk), head_cols(hh)]
        k_aug = jnp.concatenate([kj, bias_ref[j]], axis=1)
        return lax.dot_general(k_aug, qa_ref[hh], _TRANS_B, preferred_element_type=F32)

    n_iota = lax.broadcasted_iota(jnp.int32, (nb, blk), 0)
    past = n_iota < i
    k_idx = lax.broadcasted_iota(jnp.int32, (blk, blk), 0)
    q_idx = lax.broadcasted_iota(jnp.int32, (blk, blk), 1)
    causal = k_idx <= q_idx

    def stage1(hh, slot, s):
        s_ref[slot, hh] = s
        mx_ref[slot, hh] = jnp.max(s, axis=0, keepdims=True)

    def stage2(hh, slot, j):
        m_run = m_ref[hh]
        m_new = jnp.maximum(m_run, mx_ref[slot, hh])
        alpha = jnp.exp2(m_run - m_new)
        p = jnp.exp2(s_ref[slot, hh] - m_new).astype(BF16)
        m_ref[hh] = m_new
        acc_ref[hh] = alpha * acc_ref[hh] + jnp.dot(vt_ref[hh, j], p, preferred_element_type=F32)

    gate_scores = []
    for hh in range(heads):
        q = q_ref[:, head_cols(hh)]
        km = kmean_ref[:, head_cols(hh)]
        km_hi = km.astype(BF16)
        km_lo = (km - km_hi.astype(F32)).astype(BF16)
        sc = (lax.dot_general(km_hi, q, _TRANS_B, preferred_element_type=F32)
              + lax.dot_general(km_lo, q, _TRANS_B, preferred_element_type=F32))
        gate_scores.append(jnp.where(past, sc, NEG))

    for hh in range(heads):
        k_own = k_ref[pl.ds(pl.multiple_of(i * blk, blk), blk), head_cols(hh)]
        s_own = lax.dot_general(k_own, q_ref[:, head_cols(hh)], _TRANS_B, preferred_element_type=F32)
        stage1(hh, 0, jnp.where(causal, s_own, NEG))
        m_ref[hh] = jnp.full((1, blk), NEG, F32)
        acc_ref[hh] = jnp.zeros((dh + _SUM_ROWS, blk), F32)

    for hh in range(heads):
        work = gate_scores[hh]
        picked = jnp.zeros((nb, blk), jnp.bool_)
        for _ in range(topk):
            best = jnp.max(work, axis=0, keepdims=True)
            first = jnp.min(jnp.where(work == best, n_iota, nb), axis=0, keepdims=True)
            pick = n_iota == first
            picked = picked | pick
            work = jnp.where(pick, -jnp.inf, work)
        unselected = jnp.where(past & ~picked, 1.0, 0.0)
        unselected = jnp.concatenate([unselected, jnp.zeros((dh - nb, blk), F32)], axis=0)
        qa_ref[hh, :, 0:dh] = q_ref[:, head_cols(hh)]
        qa_ref[hh, :, dh:2 * dh] = unselected.T.astype(BF16)

    def visit(t, cur):
        j_cur = jnp.where(t == 0, i, t - 1)
        for hh in range(heads):
            stage1(hh, 1 - cur, scores(hh, t))
            stage2(hh, cur, j_cur)

    def visit_pair(u, carry):
        visit(2 * u, 0)
        visit(2 * u + 1, 1)
        return carry

    lax.fori_loop(0, i // 2, visit_pair, 0)

    j_last = jnp.where(i == 0, i, i - 1)

    @pl.when(i % 2 == 1)
    def _():
        visit(i - 1, 0)
        for hh in range(heads):
            stage2(hh, 1, j_last)

    @pl.when(i % 2 == 0)
    def _():
        for hh in range(heads):
            stage2(hh, 0, j_last)

    y = h_ref[...]
    for hh in range(heads):
        o = (acc_ref[hh, 0:dh, :] / acc_ref[hh, dh:dh + 1, :]).T
        og_ref[:, head_cols(hh)] = (o * _silu(gate_ref[:, head_cols(hh)])).astype(og_ref.dtype)
        if hh % 2 == 1:
            pair = slice((hh - 1) * dh, (hh + 1) * dh)
            y = y + jnp.dot(og_ref[:, pair], w_out_ref[pair, :], preferred_element_type=F32)
    if final_g_ref is not None:
        y = _rmsnorm(y, final_g_ref[...])
    o_ref[...] = y


def _moba_attn_kernel_plain(q_ref, k_ref, v_ref, gate_ref, h_ref, w_out_ref, o_ref, *scratch, **static):
    _moba_attn_kernel(q_ref, k_ref, v_ref, gate_ref, h_ref, w_out_ref, None, o_ref, *scratch, **static)


def _moba_attention(q, k, v, gate, h, w_out, final_g=None, *, batch, seq):
    n, width = q.shape
    d = h.shape[1]
    heads = ATT_HEADS
    assert heads % 2 == 0
    dh = width // heads
    blk = MOBA_BLOCK
    assert seq % blk == 0
    nb = seq // blk
    assert nb <= dh
    qspec = pl.BlockSpec((blk, width), lambda b, i: (b * nb + i, 0))
    hspec = pl.BlockSpec((blk, d), lambda b, i: (b * nb + i, 0))
    kspec = pl.BlockSpec((seq, width), lambda b, i: (b, 0))
    vspec = pl.BlockSpec((seq, width), lambda b, i: (b, 0), pipeline_mode=pl.Buffered(1))
    static = dict(blk=blk, nb=nb, topk=MOBA_TOPK, heads=heads, dh=dh)
    in_specs = [qspec, kspec, vspec, qspec, hspec, _const_spec(w_out.shape)]
    args = [q, k, v, gate, h, w_out]
    if final_g is None:
        body = functools.partial(_moba_attn_kernel_plain, **static)
    else:
        body = functools.partial(_moba_attn_kernel, **static)
        in_specs.append(_const_spec((1, d)))
        args.append(final_g)
    return pl.pallas_call(
        body,
        grid=(batch, nb),
        in_specs=in_specs,
        out_specs=hspec,
        out_shape=jax.ShapeDtypeStruct((n, d), F32),
        scratch_shapes=[
            pltpu.VMEM((nb, width), F32),
            pltpu.VMEM((heads, nb, dh + _SUM_ROWS, blk), BF16),
            pltpu.VMEM((nb, blk, dh), BF16),
            pltpu.VMEM((heads, blk, 2 * dh), BF16),
            pltpu.VMEM((2, heads, blk, blk), F32),
            pltpu.VMEM((2, heads, 1, blk), F32),
            pltpu.VMEM((heads, 1, blk), F32),
            pltpu.VMEM((heads, dh + _SUM_ROWS, blk), F32),
            pltpu.VMEM((blk, width), BF16),
        ],
        compiler_params=_params(2),
        name="moba_attention",
    )(*args)


_SUBLANES = 8
_LANES = 128
_SEG_PAD = 4
_CONV_BLOCK = 8


def _conformer_kernel(h_ref, g_ref, w_in_ref, cw_ref, cb_ref, lng_ref, lnb_ref, w_out_ref, o_ref,
                      y_ref, hist_ref, conv_ref, *, tiles_per_seq, kw):
    tm, c = o_ref.shape
    seg = tm // _SUBLANES
    pitch = seg + _SEG_PAD
    halo = kw - 1
    slabs = c // _LANES

    @pl.when(pl.program_id(0) % tiles_per_seq == 0)
    def _():
        hist_ref[...] = jnp.zeros_like(hist_ref)

    h = h_ref[...]
    xn = _rmsnorm(h, g_ref[...]).astype(BF16)
    a = jnp.dot(xn, w_in_ref[:, 0:c], preferred_element_type=F32)
    b = jnp.dot(xn, w_in_ref[:, c:2 * c], preferred_element_type=F32)
    y = a * _sigmoid(b)
    for lc in range(slabs):
        for s in range(_SUBLANES):
            y_ref[lc, s * pitch:s * pitch + seg, :] = y[s * seg:(s + 1) * seg, lc * _LANES:(lc + 1) * _LANES]

    first_segment = lax.broadcasted_iota(jnp.int32, (_SUBLANES, _LANES), 0) == 0
    conv_cols = []
    for lc in range(slabs):
        cols = slice(lc * _LANES, (lc + 1) * _LANES)
        rows = {u: y_ref[lc, pl.ds(u, _SUBLANES, stride=pitch), :] for u in range(seg)}
        for k in range(1, halo + 1):
            before = pltpu.roll(rows[seg - k], 1, 0)
            rows[-k] = jnp.where(first_segment, hist_ref[lc, seg - k:seg - k + 1, :], before)
        for u0 in range(0, seg, _CONV_BLOCK):
            steps = range(u0, min(u0 + _CONV_BLOCK, seg))
            accs = {u: jnp.broadcast_to(cb_ref[:, cols], (_SUBLANES, _LANES)) for u in steps}
            for j0 in range(0, kw, _CONV_BLOCK):
                for j in range(j0, min(j0 + _CONV_BLOCK, kw)):
                    tap = cw_ref[j:j + 1, cols]
                    for u in steps:
                        accs[u] = accs[u] + tap * rows[u - halo + j]
            for u in steps:
                conv_ref[lc, pl.ds(u, _SUBLANES, stride=pitch), :] = accs[u]
        hist_ref[lc] = y[tm - seg:tm, cols]
        conv_cols.append(jnp.concatenate(
            [conv_ref[lc, s * pitch:s * pitch + seg, :] for s in range(_SUBLANES)], axis=0))

    acc = jnp.concatenate(conv_cols, axis=1)
    mu = jnp.mean(acc, axis=-1, keepdims=True)
    cen = acc - mu
    var = jnp.mean(cen * cen, axis=-1, keepdims=True)
    y = cen * lax.rsqrt(var + EPS) * lng_ref[...] + lnb_ref[...]
    gate = jnp.dot(xn, w_in_ref[:, 2 * c:3 * c], preferred_element_type=F32)
    z = (_silu(y) * _silu(gate)).astype(BF16)
    o_ref[...] = h + jnp.dot(z, w_out_ref[...], preferred_element_type=F32)


def _conformer_layer(h, g, w_in, conv_w, conv_b, ln_g, ln_b, w_out, *, seq, tm=512):
    n, d = h.shape
    c = w_out.shape[0]
    kw = conv_w.shape[0]
    seg = tm // _SUBLANES
    assert seq % tm == 0 and tm % (_SUBLANES * _SUBLANES) == 0 and c % _LANES == 0 and kw - 1 <= seg
    seg_rows = _SUBLANES * (seg + _SEG_PAD)
    return pl.pallas_call(
        functools.partial(_conformer_kernel, tiles_per_seq=seq // tm, kw=kw),
        grid=(n // tm,),
        in_specs=[
            pl.BlockSpec((tm, d), lambda i: (i, 0)),
            _const_spec((1, d)),
            _const_spec(w_in.shape),
            _const_spec(conv_w.shape),
            _const_spec((1, c)),
            _const_spec((1, c)),
            _const_spec((1, c)),
            _const_spec(w_out.shape),
        ],
        out_specs=pl.BlockSpec((tm, d), lambda i: (i, 0)),
        out_shape=jax.ShapeDtypeStruct((n, d), F32),
        scratch_shapes=[
            pltpu.VMEM((c // _LANES, seg_rows, _LANES), F32),
            pltpu.VMEM((c // _LANES, seg, _LANES), F32),
            pltpu.VMEM((c // _LANES, seg_rows, _LANES), F32),
        ],
        compiler_params=_params(1),
        name="conformer_layer",
    )(h, g, w_in, conv_w, conv_b, ln_g, ln_b, w_out)


def _rglru_kernel(h_ref, g_ref, w_in_ref, cw_ref, cb_ref, w_gates_ref, b_rg_ref, b_ig_ref, lam_ref,
                  w_out_ref, o_ref, hseg_ref, oseg_ref, hist_ref, state_ref, *, tiles_per_seq, kw, heads):
    tm, d = o_ref.shape
    w = w_out_ref.shape[0]
    hd = w // heads
    seg = tm // _SUBLANES
    pitch = seg + _SEG_PAD
    halo = kw - 1
    slabs = d // _LANES

    @pl.when(pl.program_id(0) % tiles_per_seq == 0)
    def _():
        hist_ref[...] = jnp.zeros_like(hist_ref)
        state_ref[...] = jnp.zeros_like(state_ref)

    def step(x, u):
        return x[u * _SUBLANES:(u + 1) * _SUBLANES, :]

    for lc in range(slabs):
        for s in range(_SUBLANES):
            hseg_ref[lc, s * pitch:s * pitch + seg, :] = h_ref[s * seg:(s + 1) * seg, lc * _LANES:(lc + 1) * _LANES]
    h = jnp.concatenate(
        [jnp.concatenate([hseg_ref[lc, pl.ds(u, _SUBLANES, stride=pitch), :] for u in range(seg)], axis=0)
         for lc in range(slabs)], axis=1)

    xn = _rmsnorm(h, g_ref[...]).astype(BF16)
    xb = jnp.dot(xn, w_in_ref[:, 0:w], preferred_element_type=F32)

    first_segment = lax.broadcasted_iota(jnp.int32, (_SUBLANES, w), 0) == 0
    rows = {u: step(xb, u) for u in range(seg)}
    for k in range(1, halo + 1):
        rows[-k] = jnp.where(first_segment, hist_ref[k - 1:k, :], pltpu.roll(rows[seg - k], 1, 0))
    for k in range(1, halo + 1):
        hist_ref[k - 1:k, :] = rows[seg - k][_SUBLANES - 1:_SUBLANES, :]
    conv_steps = []
    for u in range(seg):
        acc = cb_ref[...] + cw_ref[0:1, :] * rows[u - halo]
        for j in range(1, kw):
            acc = acc + cw_ref[j:j + 1, :] * rows[u - halo + j]
        conv_steps.append(acc)
    xc = jnp.concatenate(conv_steps, axis=0)

    neg_lam = -lam_ref[...]
    c_softplus = LRU_C * (jnp.maximum(neg_lam, 0.0) + jnp.log1p(jnp.exp(-jnp.abs(neg_lam))))
    neg_c_log2e = -(c_softplus * LOG2_E)
    xc_bf = xc.astype(BF16)
    hs_cols = []
    gate_cols = []
    for hh in range(heads):
        cols = slice(hh * hd, (hh + 1) * hd)
        gates = jnp.dot(xc_bf[:, cols], w_gates_ref[hh], preferred_element_type=F32)
        r = _sigmoid(gates[:, 0:hd] + b_rg_ref[:, cols])
        ig = _sigmoid(gates[:, hd:2 * hd] + b_ig_ref[:, cols])
        a = jnp.exp2(r * neg_c_log2e[:, cols])
        one_minus_a2 = jnp.tanh(r * c_softplus[:, cols]) * (a * a + 1.0)
        b = jnp.sqrt(one_minus_a2) * (ig * xc[:, cols])

        prod = jnp.ones((_SUBLANES, hd), F32)
        total = jnp.zeros((_SUBLANES, hd), F32)
        for u in range(seg):
            au = step(a, u)
            total = au * total + step(b, u)
            prod = au * prod
        cur = state_ref[:, cols]
        starts = []
        for s in range(_SUBLANES):
            starts.append(cur)
            cur = prod[s:s + 1, :] * cur + total[s:s + 1, :]
        state_ref[:, cols] = cur
        hv = jnp.concatenate(starts, axis=0)
        states = []
        for u in range(seg):
            hv = step(a, u) * hv + step(b, u)
            states.append(hv)
        hs_cols.append(jnp.concatenate(states, axis=0))
        if hh % 2 == 1:
            lo = (hh - 1) * hd
            gate_cols.append(jnp.dot(xn, w_in_ref[:, w + lo:w + lo + 2 * hd], preferred_element_type=F32))

    y = (jnp.concatenate(hs_cols, axis=1) * _silu(jnp.concatenate(gate_cols, axis=1))).astype(BF16)
    out = h + jnp.dot(y, w_out_ref[...], preferred_element_type=F32)

    for lc in range(slabs):
        cols = slice(lc * _LANES, (lc + 1) * _LANES)
        for u in range(seg):
            oseg_ref[lc, pl.ds(u, _SUBLANES, stride=pitch), :] = step(out[:, cols], u)
        for s in range(_SUBLANES):
            o_ref[s * seg:(s + 1) * seg, cols] = oseg_ref[lc, s * pitch:s * pitch + seg, :]


def _rglru_layer(h, g, w_in, conv_w, conv_b, w_gates, b_rg, b_ig, lam, w_out, *, seq, tm=512):
    n, d = h.shape
    w = w_out.shape[0]
    kw = conv_w.shape[0]
    seg = tm // _SUBLANES
    assert seq % tm == 0 and tm % (_SUBLANES * _SUBLANES) == 0 and d % _LANES == 0 and kw - 1 <= _SUBLANES
    assert LRU_HEADS % 2 == 0
    seg_rows = _SUBLANES * (seg + _SEG_PAD)
    return pl.pallas_call(
        functools.partial(_rglru_kernel, tiles_per_seq=seq // tm, kw=kw, heads=LRU_HEADS),
        grid=(n // tm,),
        in_specs=[
            pl.BlockSpec((tm, d), lambda i: (i, 0)),
            _const_spec((1, d)),
            _const_spec(w_in.shape),
            _const_spec(conv_w.shape),
            _const_spec((1, w)),
            _const_spec(w_gates.shape),
            _const_spec((1, w)),
            _const_spec((1, w)),
            _const_spec((1, w)),
            _const_spec(w_out.shape),
        ],
        out_specs=pl.BlockSpec((tm, d), lambda i: (i, 0)),
        out_shape=jax.ShapeDtypeStruct((n, d), F32),
        scratch_shapes=[
            pltpu.VMEM((d // _LANES, seg_rows, _LANES), F32),
            pltpu.VMEM((d // _LANES, seg_rows, _LANES), F32),
            pltpu.VMEM((_SUBLANES, w), F32),
            pltpu.VMEM((1, w), F32),
        ],
        compiler_params=_params(1),
        name="rglru_layer",
    )(h, g, w_in, conv_w, conv_b, w_gates, b_rg, b_ig, lam, w_out)


def _moba_layer(h, norm_g, w_in, w_out, *, batch, seq, final_g=None):
    head_dim = w_out.shape[0] // ATT_HEADS
    q, k, v, gate = _rms_qkvg(h, norm_g, w_in.astype(BF16), head_dim=head_dim)
    return _moba_attention(q, k, v, gate, h, w_out.astype(BF16), final_g, batch=batch, seq=seq)


def _row(p):
    return p.reshape(1, -1)


def kernel(x, l0_norm_g, l0_w_in, l0_w_out, l1_norm_g, l1_w_in, l1_conv_w, l1_conv_b, l1_ln_g, l1_ln_b, l1_w_out, l2_norm_g, l2_w_in, l2_conv_w, l2_conv_b, l2_w_rg, l2_b_rg, l2_w_ig, l2_b_ig, l2_lam, l2_w_out, l3_norm_g, l3_w_in, l3_w_out, final_g):
    batch, seq, d = x.shape
    h = x.reshape(batch * seq, d)

    h = _moba_layer(h, _row(l0_norm_g), l0_w_in, l0_w_out, batch=batch, seq=seq)

    h = _conformer_layer(h, _row(l1_norm_g), l1_w_in.astype(BF16), l1_conv_w, _row(l1_conv_b),
                         _row(l1_ln_g), _row(l1_ln_b), l1_w_out.astype(BF16), seq=seq)

    w_gates = jnp.concatenate([l2_w_rg, l2_w_ig], axis=-1).astype(BF16)
    h = _rglru_layer(h, _row(l2_norm_g), l2_w_in.astype(BF16), l2_conv_w, _row(l2_conv_b), w_gates,
                     _row(l2_b_rg), _row(l2_b_ig), _row(l2_lam), l2_w_out.astype(BF16), seq=seq)

    out = _moba_layer(h, _row(l3_norm_g), l3_w_in, l3_w_out, batch=batch, seq=seq, final_g=_row(final_g))
    return out.reshape(batch, seq, d)
```

```python
import functools

import jax
import jax.numpy as jnp
from jax import lax
from jax.experimental import pallas as pl
from jax.experimental.pallas import tpu as pltpu

F32 = jnp.float32
BF16 = jnp.bfloat16

EPS = 1e-6
NEG = -1e30

ATT_HEADS = 8
MOBA_BLOCK = 256
MOBA_TOPK = 3
LRU_HEADS = 10
LRU_C = 8.0

VMEM_LIMIT_BYTES = 56 * 1024 * 1024
LOG2_E = 1.4426950408889634


def _const_spec(shape):
    zeros = (0,) * len(shape)
    return pl.BlockSpec(shape, lambda *_: zeros, pipeline_mode=pl.Buffered(1))


def _params(n_axes):
    return pltpu.CompilerParams(
        dimension_semantics=("arbitrary",) * n_axes,
        vmem_limit_bytes=VMEM_LIMIT_BYTES,
    )


def _rmsnorm(x, g):
    return x * lax.rsqrt(jnp.mean(x * x, axis=-1, keepdims=True) + EPS) * g


def _sigmoid(x):
    return 0.5 * jnp.tanh(0.5 * x) + 0.5


def _silu(x):
    u = 0.5 * x
    return u * jnp.tanh(u) + u


_SUM_ROWS = 16


def _rms_qkvg_kernel(x_ref, g_ref, w_ref, q_ref, k_ref, vt_ref, gate_ref, *, q_scale):
    xn = _rmsnorm(x_ref[...], g_ref[...]).astype(BF16)
    width = q_ref.shape[1]
    blocks, heads, rows, blk = vt_ref.shape
    dh = rows - _SUM_ROWS

    def project(c):
        return jnp.dot(xn, w_ref[:, c * width:(c + 1) * width], preferred_element_type=F32)

    v = project(2)
    for jb in range(blocks):
        for hh in range(heads):
            vt_ref[jb, hh, 0:dh, :] = v[jb * blk:(jb + 1) * blk, hh * dh:(hh + 1) * dh].T.astype(BF16)
            vt_ref[jb, hh, dh:rows, :] = jnp.ones((_SUM_ROWS, blk), BF16)
    q_ref[...] = (project(0) * q_scale).astype(q_ref.dtype)
    k_ref[...] = project(1).astype(k_ref.dtype)
    gate_ref[...] = project(3)


def _rms_qkvg(h, g, w_in, *, head_dim, tm=1024):
    n, d = h.shape
    width = w_in.shape[1] // 4
    heads = width // head_dim
    blk = MOBA_BLOCK
    assert tm % blk == 0
    out_bf = jax.ShapeDtypeStruct((n, width), BF16)
    row_spec = pl.BlockSpec((tm, width), lambda i: (i, 0))
    vt_shape = (n // blk, heads, head_dim + _SUM_ROWS, blk)
    vt_spec = pl.BlockSpec((tm // blk,) + vt_shape[1:], lambda i: (i, 0, 0, 0))
    return pl.pallas_call(
        functools.partial(_rms_qkvg_kernel, q_scale=head_dim ** -0.5 * LOG2_E),
        grid=(n // tm,),
        in_specs=[
            pl.BlockSpec((tm, d), lambda i: (i, 0)),
            _const_spec((1, d)),
            _const_spec(w_in.shape),
        ],
        out_specs=[row_spec, row_spec, vt_spec, row_spec],
        out_shape=[out_bf, out_bf, jax.ShapeDtypeStruct(vt_shape, BF16), jax.ShapeDtypeStruct((n, width), F32)],
        compiler_params=_params(1),
        name="rms_qkvg",
    )(h, g, w_in)


_TRANS_B = (((1,), (1,)), ((), ()))
_TRANS_A = (((0,), (0,)), ((), ()))


def _moba_attn_kernel(q_ref, k_ref, vt_ref, gate_ref, h_ref, w_out_ref, final_g_ref, o_ref, kmean_ref,
                      bias_ref, qa_ref, s_ref, mx_ref, m_ref, acc_ref, og_ref, *, blk, nb, topk, heads, dh):
    i = pl.program_id(1)

    def head_cols(hh):
        return slice(hh * dh, (hh + 1) * dh)

    @pl.when(i == 0)
    def _():
        lane = lax.broadcasted_iota(jnp.int32, (blk, dh), 1)

        def prep_block(j, carry):
            rows = pl.ds(pl.multiple_of(j * blk, blk), blk)
            kmean_ref[pl.ds(j, 1), :] = jnp.mean(k_ref[rows, :].astype(F32), axis=0, keepdims=True)
            bias_ref[j] = jnp.where(lane == j, NEG, 0.0).astype(BF16)
            return carry

        lax.fori_loop(0, nb, prep_block, 0)

    def scores(hh, j):
        kj = k_ref[pl.ds(pl.multiple_of(j * blk, blk), blk), head_cols(hh)]
        k_aug = jnp.concatenate([kj, bias_ref[j]], axis=1)
        return lax.dot_general(k_aug, qa_ref[hh], _TRANS_B, preferred_element_type=F32)

    n_iota = lax.broadcasted_iota(jnp.int32, (nb, blk), 0)
    past = n_iota < i
    k_idx = lax.broadcasted_iota(jnp.int32, (blk, blk), 0)
    q_idx = lax.broadcasted_iota(jnp.int32, (blk, blk), 1)
    causal = k_idx <= q_idx

    def stage1(hh, slot, s):
        s_ref[slot, hh] = s
        mx_ref[slot, hh] = jnp.max(s, axis=0, keepdims=True)

    def stage2(hh, slot, j):
        m_run = m_ref[hh]
        m_new = jnp.maximum(m_run, mx_ref[slot, hh])
        alpha = jnp.exp2(m_run - m_new)
        p = jnp.exp2(s_ref[slot, hh] - m_new).astype(BF16)
        m_ref[hh] = m_new
        acc_ref[hh] = alpha * acc_ref[hh] + jnp.dot(vt_ref[j, hh], p, preferred_element_type=F32)

    gate_scores = []
    for hh in range(heads):
        q = q_ref[:, head_cols(hh)]
        km = kmean_ref[:, head_cols(hh)]
        km_hi = km.astype(BF16)
        km_lo = (km - km_hi.astype(F32)).astype(BF16)
        sc = (lax.dot_general(km_hi, q, _TRANS_B, preferred_element_type=F32)
              + lax.dot_general(km_lo, q, _TRANS_B, preferred_element_type=F32))
        gate_scores.append(jnp.where(past, sc, NEG))

    for hh in range(heads):
        k_own = k_ref[pl.ds(pl.multiple_of(i * blk, blk), blk), head_cols(hh)]
        s_own = lax.dot_general(k_own, q_ref[:, head_cols(hh)], _TRANS_B, preferred_element_type=F32)
        stage1(hh, 0, jnp.where(causal, s_own, NEG))
        m_ref[hh] = jnp.full((1, blk), NEG, F32)
        acc_ref[hh] = jnp.zeros((dh + _SUM_ROWS, blk), F32)

    for hh in range(heads):
        work = gate_scores[hh]
        picked = jnp.zeros((nb, blk), jnp.bool_)
        for _ in range(topk):
            best = jnp.max(work, axis=0, keepdims=True)
            first = jnp.min(jnp.where(work == best, n_iota, nb), axis=0, keepdims=True)
            pick = n_iota == first
            picked = picked | pick
            work = jnp.where(pick, -jnp.inf, work)
        unselected = jnp.where(past & ~picked, 1.0, 0.0)
        unselected = jnp.concatenate([unselected, jnp.zeros((dh - nb, blk), F32)], axis=0)
        qa_ref[hh, :, 0:dh] = q_ref[:, head_cols(hh)]
        qa_ref[hh, :, dh:2 * dh] = unselected.T.astype(BF16)

    def visit(t, cur):
        j_cur = jnp.where(t == 0, i, t - 1)
        for hh in range(heads):
            stage1(hh, 1 - cur, scores(hh, t))
            stage2(hh, cur, j_cur)

    def visit_pair(u, carry):
        visit(2 * u, 0)
        visit(2 * u + 1, 1)
        return carry

    lax.fori_loop(0, i // 2, visit_pair, 0)

    j_last = jnp.where(i == 0, i, i - 1)

    @pl.when(i % 2 == 1)
    def _():
        visit(i - 1, 0)
        for hh in range(heads):
            stage2(hh, 1, j_last)

    @pl.when(i % 2 == 0)
    def _():
        for hh in range(heads):
            stage2(hh, 0, j_last)

    y = h_ref[...]
    for hh in range(heads):
        o = (acc_ref[hh, 0:dh, :] / acc_ref[hh, dh:dh + 1, :]).T
        og_ref[:, head_cols(hh)] = (o * _silu(gate_ref[:, head_cols(hh)])).astype(og_ref.dtype)
        if hh % 2 == 1:
            pair = slice((hh - 1) * dh, (hh + 1) * dh)
            y = y + jnp.dot(og_ref[:, pair], w_out_ref[pair, :], preferred_element_type=F32)
    if final_g_ref is not None:
        y = _rmsnorm(y, final_g_ref[...])
    o_ref[...] = y


def _moba_attn_kernel_plain(q_ref, k_ref, vt_ref, gate_ref, h_ref, w_out_ref, o_ref, *scratch, **static):
    _moba_attn_kernel(q_ref, k_ref, vt_ref, gate_ref, h_ref, w_out_ref, None, o_ref, *scratch, **static)


def _moba_attention(q, k, vt, gate, h, w_out, final_g=None, *, batch, seq):
    n, width = q.shape
    d = h.shape[1]
    heads = ATT_HEADS
    assert heads % 2 == 0
    dh = width // heads
    blk = MOBA_BLOCK
    assert seq % blk == 0
    nb = seq // blk
    assert nb <= dh
    qspec = pl.BlockSpec((blk, width), lambda b, i: (b * nb + i, 0))
    hspec = pl.BlockSpec((blk, d), lambda b, i: (b * nb + i, 0))
    kspec = pl.BlockSpec((seq, width), lambda b, i: (b, 0))
    vtspec = pl.BlockSpec((nb,) + vt.shape[1:], lambda b, i: (b, 0, 0, 0))
    static = dict(blk=blk, nb=nb, topk=MOBA_TOPK, heads=heads, dh=dh)
    in_specs = [qspec, kspec, vtspec, qspec, hspec, _const_spec(w_out.shape)]
    args = [q, k, vt, gate, h, w_out]
    if final_g is None:
        body = functools.partial(_moba_attn_kernel_plain, **static)
    else:
        body = functools.partial(_moba_attn_kernel, **static)
        in_specs.append(_const_spec((1, d)))
        args.append(final_g)
    return pl.pallas_call(
        body,
        grid=(batch, nb),
        in_specs=in_specs,
        out_specs=hspec,
        out_shape=jax.ShapeDtypeStruct((n, d), F32),
        scratch_shapes=[
            pltpu.VMEM((nb, width), F32),
            pltpu.VMEM((nb, blk, dh), BF16),
            pltpu.VMEM((heads, blk, 2 * dh), BF16),
            pltpu.VMEM((2, heads, blk, blk), F32),
            pltpu.VMEM((2, heads, 1, blk), F32),
            pltpu.VMEM((heads, 1, blk), F32),
            pltpu.VMEM((heads, dh + _SUM_ROWS, blk), F32),
            pltpu.VMEM((blk, width), BF16),
        ],
        compiler_params=_params(2),
        name="moba_attention",
    )(*args)


_SUBLANES = 8
_LANES = 128
_SEG_PAD = 4
_CONV_BLOCK = 8


def _conformer_kernel(h_ref, g_ref, w_in_ref, cw_ref, cb_ref, lng_ref, lnb_ref, w_out_ref, o_ref,
                      y_ref, hist_ref, conv_ref, *, tiles_per_seq, kw):
    tm, c = o_ref.shape
    seg = tm // _SUBLANES
    pitch = seg + _SEG_PAD
    halo = kw - 1
    slabs = c // _LANES

    @pl.when(pl.program_id(0) % tiles_per_seq == 0)
    def _():
        hist_ref[...] = jnp.zeros_like(hist_ref)

    h = h_ref[...]
    xn = _rmsnorm(h, g_ref[...]).astype(BF16)
    a = jnp.dot(xn, w_in_ref[:, 0:c], preferred_element_type=F32)
    b = jnp.dot(xn, w_in_ref[:, c:2 * c], preferred_element_type=F32)
    y = a * _sigmoid(b)
    for lc in range(slabs):
        for s in range(_SUBLANES):
            y_ref[lc, s * pitch:s * pitch + seg, :] = y[s * seg:(s + 1) * seg, lc * _LANES:(lc + 1) * _LANES]

    first_segment = lax.broadcasted_iota(jnp.int32, (_SUBLANES, _LANES), 0) == 0
    conv_cols = []
    for lc in range(slabs):
        cols = slice(lc * _LANES, (lc + 1) * _LANES)
        rows = {u: y_ref[lc, pl.ds(u, _SUBLANES, stride=pitch), :] for u in range(seg)}
        for k in range(1, halo + 1):
            before = pltpu.roll(rows[seg - k], 1, 0)
            rows[-k] = jnp.where(first_segment, hist_ref[lc, seg - k:seg - k + 1, :], before)
        for u0 in range(0, seg, _CONV_BLOCK):
            steps = range(u0, min(u0 + _CONV_BLOCK, seg))
            accs = {u: jnp.broadcast_to(cb_ref[:, cols], (_SUBLANES, _LANES)) for u in steps}
            for j0 in range(0, kw, _CONV_BLOCK):
                for j in range(j0, min(j0 + _CONV_BLOCK, kw)):
                    tap = cw_ref[j:j + 1, cols]
                    for u in steps:
                        accs[u] = accs[u] + tap * rows[u - halo + j]
            for u in steps:
                conv_ref[lc, pl.ds(u, _SUBLANES, stride=pitch), :] = accs[u]
        hist_ref[lc] = y[tm - seg:tm, cols]
        conv_cols.append(jnp.concatenate(
            [conv_ref[lc, s * pitch:s * pitch + seg, :] for s in range(_SUBLANES)], axis=0))

    acc = jnp.concatenate(conv_cols, axis=1)
    mu = jnp.mean(acc, axis=-1, keepdims=True)
    cen = acc - mu
    var = jnp.mean(cen * cen, axis=-1, keepdims=True)
    y = cen * lax.rsqrt(var + EPS) * lng_ref[...] + lnb_ref[...]
    gate = jnp.dot(xn, w_in_ref[:, 2 * c:3 * c], preferred_element_type=F32)
    z = (_silu(y) * _silu(gate)).astype(BF16)
    o_ref[...] = h + jnp.dot(z, w_out_ref[...], preferred_element_type=F32)


def _conformer_layer(h, g, w_in, conv_w, conv_b, ln_g, ln_b, w_out, *, seq, tm=1024):
    n, d = h.shape
    c = w_out.shape[0]
    kw = conv_w.shape[0]
    seg = tm // _SUBLANES
    assert seq % tm == 0 and tm % (_SUBLANES * _SUBLANES) == 0 and c % _LANES == 0 and kw - 1 <= seg
    seg_rows = _SUBLANES * (seg + _SEG_PAD)
    return pl.pallas_call(
        functools.partial(_conformer_kernel, tiles_per_seq=seq // tm, kw=kw),
        grid=(n // tm,),
        in_specs=[
            pl.BlockSpec((tm, d), lambda i: (i, 0)),
            _const_spec((1, d)),
            _const_spec(w_in.shape),
            _const_spec(conv_w.shape),
            _const_spec((1, c)),
            _const_spec((1, c)),
            _const_spec((1, c)),
            _const_spec(w_out.shape),
        ],
        out_specs=pl.BlockSpec((tm, d), lambda i: (i, 0)),
        out_shape=jax.ShapeDtypeStruct((n, d), F32),
        scratch_shapes=[
            pltpu.VMEM((c // _LANES, seg_rows, _LANES), F32),
            pltpu.VMEM((c // _LANES, seg, _LANES), F32),
            pltpu.VMEM((c // _LANES, seg_rows, _LANES), F32),
        ],
        compiler_params=_params(1),
        name="conformer_layer",
    )(h, g, w_in, conv_w, conv_b, ln_g, ln_b, w_out)


def _rglru_kernel(h_ref, g_ref, w_in_ref, cw_ref, cb_ref, w_gates_ref, b_rg_ref, b_ig_ref, lam_ref,
                  w_out_ref, o_ref, hseg_ref, oseg_ref, hist_ref, state_ref, *, tiles_per_seq, kw, heads):
    tm, d = o_ref.shape
    w = w_out_ref.shape[0]
    hd = w // heads
    seg = tm // _SUBLANES
    pitch = seg + _SEG_PAD
    halo = kw - 1
    slabs = d // _LANES

    @pl.when(pl.program_id(0) % tiles_per_seq == 0)
    def _():
        hist_ref[...] = jnp.zeros_like(hist_ref)
        state_ref[...] = jnp.zeros_like(state_ref)

    def step(x, u):
        return x[u * _SUBLANES:(u + 1) * _SUBLANES, :]

    for lc in range(slabs):
        for s in range(_SUBLANES):
            hseg_ref[lc, s * pitch:s * pitch + seg, :] = h_ref[s * seg:(s + 1) * seg, lc * _LANES:(lc + 1) * _LANES]
    h = jnp.concatenate(
        [jnp.concatenate([hseg_ref[lc, pl.ds(u, _SUBLANES, stride=pitch), :] for u in range(seg)], axis=0)
         for lc in range(slabs)], axis=1)

    xn = _rmsnorm(h, g_ref[...]).astype(BF16)
    xb = jnp.dot(xn, w_in_ref[:, 0:w], preferred_element_type=F32)

    first_segment = lax.broadcasted_iota(jnp.int32, (_SUBLANES, w), 0) == 0
    rows = {u: step(xb, u) for u in range(seg)}
    for k in range(1, halo + 1):
        rows[-k] = jnp.where(first_segment, hist_ref[k - 1:k, :], pltpu.roll(rows[seg - k], 1, 0))
    for k in range(1, halo + 1):
        hist_ref[k - 1:k, :] = rows[seg - k][_SUBLANES - 1:_SUBLANES, :]
    conv_steps = []
    for u in range(seg):
        acc = cb_ref[...] + cw_ref[0:1, :] * rows[u - halo]
        for j in range(1, kw):
            acc = acc + cw_ref[j:j + 1, :] * rows[u - halo + j]
        conv_steps.append(acc)
    xc = jnp.concatenate(conv_steps, axis=0)

    neg_lam = -lam_ref[...]
    c_softplus = LRU_C * (jnp.maximum(neg_lam, 0.0) + jnp.log1p(jnp.exp(-jnp.abs(neg_lam))))
    neg_c_log2e = -(c_softplus * LOG2_E)
    xc_bf = xc.astype(BF16)
    hs_cols = []
    gate_cols = []
    for hh in range(heads):
        cols = slice(hh * hd, (hh + 1) * hd)
        gates = jnp.dot(xc_bf[:, cols], w_gates_ref[hh], preferred_element_type=F32)
        r = _sigmoid(gates[:, 0:hd] + b_rg_ref[:, cols])
        ig = _sigmoid(gates[:, hd:2 * hd] + b_ig_ref[:, cols])
        a = jnp.exp2(r * neg_c_log2e[:, cols])
        one_minus_a2 = jnp.tanh(r * c_softplus[:, cols]) * (a * a + 1.0)
        b = jnp.sqrt(one_minus_a2) * (ig * xc[:, cols])

        prod = jnp.ones((_SUBLANES, hd), F32)
        total = jnp.zeros((_SUBLANES, hd), F32)
        for u in range(seg):
            au = step(a, u)
            total = au * total + step(b, u)
            prod = au * prod
        cur = state_ref[:, cols]
        starts = []
        for s in range(_SUBLANES):
            starts.append(cur)
            cur = prod[s:s + 1, :] * cur + total[s:s + 1, :]
        state_ref[:, cols] = cur
        hv = jnp.concatenate(starts, axis=0)
        states = []
        for u in range(seg):
            hv = step(a, u) * hv + step(b, u)
            states.append(hv)
        hs_cols.append(jnp.concatenate(states, axis=0))
        if hh % 2 == 1:
            lo = (hh - 1) * hd
            gate_cols.append(jnp.dot(xn, w_in_ref[:, w + lo:w + lo + 2 * hd], preferred_element_type=F32))

    y = (jnp.concatenate(hs_cols, axis=1) * _silu(jnp.concatenate(gate_cols, axis=1))).astype(BF16)
    out = h + jnp.dot(y, w_out_ref[...], preferred_element_type=F32)

    for lc in range(slabs):
        cols = slice(lc * _LANES, (lc + 1) * _LANES)
        for u in range(seg):
            oseg_ref[lc, pl.ds(u, _SUBLANES, stride=pitch), :] = step(out[:, cols], u)
        for s in range(_SUBLANES):
            o_ref[s * seg:(s + 1) * seg, cols] = oseg_ref[lc, s * pitch:s * pitch + seg, :]


def _rglru_layer(h, g, w_in, conv_w, conv_b, w_gates, b_rg, b_ig, lam, w_out, *, seq, tm=1024):
    n, d = h.shape
    w = w_out.shape[0]
    kw = conv_w.shape[0]
    seg = tm // _SUBLANES
    assert seq % tm == 0 and tm % (_SUBLANES * _SUBLANES) == 0 and d % _LANES == 0 and kw - 1 <= _SUBLANES
    assert LRU_HEADS % 2 == 0
    seg_rows = _SUBLANES * (seg + _SEG_PAD)
    return pl.pallas_call(
        functools.partial(_rglru_kernel, tiles_per_seq=seq // tm, kw=kw, heads=LRU_HEADS),
        grid=(n // tm,),
        in_specs=[
            pl.BlockSpec((tm, d), lambda i: (i, 0)),
            _const_spec((1, d)),
            _const_spec(w_in.shape),
            _const_spec(conv_w.shape),
            _const_spec((1, w)),
            _const_spec(w_gates.shape),
            _const_spec((1, w)),
            _const_spec((1, w)),
            _const_spec((1, w)),
            _const_spec(w_out.shape),
        ],
        out_specs=pl.BlockSpec((tm, d), lambda i: (i, 0)),
        out_shape=jax.ShapeDtypeStruct((n, d), F32),
        scratch_shapes=[
            pltpu.VMEM((d // _LANES, seg_rows, _LANES), F32),
            pltpu.VMEM((d // _LANES, seg_rows, _LANES), F32),
            pltpu.VMEM((_SUBLANES, w), F32),
            pltpu.VMEM((1, w), F32),
        ],
        compiler_params=_params(1),
        name="rglru_layer",
    )(h, g, w_in, conv_w, conv_b, w_gates, b_rg, b_ig, lam, w_out)


def _moba_layer(h, norm_g, w_in, w_out, *, batch, seq, final_g=None):
    head_dim = w_out.shape[0] // ATT_HEADS
    q, k, vt, gate = _rms_qkvg(h, norm_g, w_in.astype(BF16), head_dim=head_dim)
    return _moba_attention(q, k, vt, gate, h, w_out.astype(BF16), final_g, batch=batch, seq=seq)


def _row(p):
    return p.reshape(1, -1)


def kernel(x, l0_norm_g, l0_w_in, l0_w_out, l1_norm_g, l1_w_in, l1_conv_w, l1_conv_b, l1_ln_g, l1_ln_b, l1_w_out, l2_norm_g, l2_w_in, l2_conv_w, l2_conv_b, l2_w_rg, l2_b_rg, l2_w_ig, l2_b_ig, l2_lam, l2_w_out, l3_norm_g, l3_w_in, l3_w_out, final_g):
    batch, seq, d = x.shape
    h = x.reshape(batch * seq, d)

    h = _moba_layer(h, _row(l0_norm_g), l0_w_in, l0_w_out, batch=batch, seq=seq)

    h = _conformer_layer(h, _row(l1_norm_g), l1_w_in.astype(BF16), l1_conv_w, _row(l1_conv_b),
                         _row(l1_ln_g), _row(l1_ln_b), l1_w_out.astype(BF16), seq=seq)

    w_gates = jnp.concatenate([l2_w_rg, l2_w_ig], axis=-1).astype(BF16)
    h = _rglru_layer(h, _row(l2_norm_g), l2_w_in.astype(BF16), l2_conv_w, _row(l2_conv_b), w_gates,
                     _row(l2_b_rg), _row(l2_b_ig), _row(l2_lam), l2_w_out.astype(BF16), seq=seq)

    out = _moba_layer(h, _row(l3_norm_g), l3_w_in, l3_w_out, batch=batch, seq=seq, final_g=_row(final_g))
    return out.reshape(batch, seq, d)
```

```python
import functools

import jax
import jax.numpy as jnp
from jax import lax
from jax.experimental import pallas as pl
from jax.experimental.pallas import tpu as pltpu

F32 = jnp.float32
BF16 = jnp.bfloat16

EPS = 1e-6
NEG = -1e30

ATT_HEADS = 8
MOBA_BLOCK = 256
MOBA_TOPK = 3
LRU_HEADS = 10
LRU_C = 8.0

VMEM_LIMIT_BYTES = 56 * 1024 * 1024
LOG2_E = 1.4426950408889634


def _const_spec(shape):
    zeros = (0,) * len(shape)
    return pl.BlockSpec(shape, lambda *_: zeros, pipeline_mode=pl.Buffered(1))


def _params(n_axes):
    return pltpu.CompilerParams(
        dimension_semantics=("arbitrary",) * n_axes,
        vmem_limit_bytes=VMEM_LIMIT_BYTES,
    )


def _rmsnorm(x, g):
    return x * lax.rsqrt(jnp.mean(x * x, axis=-1, keepdims=True) + EPS) * g


def _sigmoid(x):
    return 0.5 * jnp.tanh(0.5 * x) + 0.5


def _silu(x):
    u = 0.5 * x
    return u * jnp.tanh(u) + u


_SUM_ROWS = 16


def _rms_qkvg_kernel(x_ref, g_ref, w_ref, q_ref, k_ref, vt_ref, gate_ref, *, q_scale):
    xn = _rmsnorm(x_ref[...], g_ref[...]).astype(BF16)
    width = q_ref.shape[1]
    blocks, heads, rows, blk = vt_ref.shape
    dh = rows - _SUM_ROWS

    def project(c):
        return jnp.dot(xn, w_ref[:, c * width:(c + 1) * width], preferred_element_type=F32)

    v = project(2)
    for jb in range(blocks):
        for hh in range(heads):
            vt_ref[jb, hh, 0:dh, :] = v[jb * blk:(jb + 1) * blk, hh * dh:(hh + 1) * dh].T.astype(BF16)
            vt_ref[jb, hh, dh:rows, :] = jnp.ones((_SUM_ROWS, blk), BF16)
    q_ref[...] = (project(0) * q_scale).astype(q_ref.dtype)
    k_ref[...] = project(1).astype(k_ref.dtype)
    gate_ref[...] = project(3)


def _rms_qkvg(h, g, w_in, *, head_dim, tm=1024):
    n, d = h.shape
    width = w_in.shape[1] // 4
    heads = width // head_dim
    blk = MOBA_BLOCK
    assert tm % blk == 0
    out_bf = jax.ShapeDtypeStruct((n, width), BF16)
    row_spec = pl.BlockSpec((tm, width), lambda i: (i, 0))
    vt_shape = (n // blk, heads, head_dim + _SUM_ROWS, blk)
    vt_spec = pl.BlockSpec((tm // blk,) + vt_shape[1:], lambda i: (i, 0, 0, 0))
    return pl.pallas_call(
        functools.partial(_rms_qkvg_kernel, q_scale=head_dim ** -0.5 * LOG2_E),
        grid=(n // tm,),
        in_specs=[
            pl.BlockSpec((tm, d), lambda i: (i, 0)),
            _const_spec((1, d)),
            _const_spec(w_in.shape),
        ],
        out_specs=[row_spec, row_spec, vt_spec, row_spec],
        out_shape=[out_bf, out_bf, jax.ShapeDtypeStruct(vt_shape, BF16), jax.ShapeDtypeStruct((n, width), F32)],
        compiler_params=_params(1),
        name="rms_qkvg",
    )(h, g, w_in)


_TRANS_B = (((1,), (1,)), ((), ()))
_TRANS_A = (((0,), (0,)), ((), ()))
_PROJ_HEADS = 4


def _moba_attn_kernel(q_ref, k_ref, vt_ref, gate_ref, h_ref, w_out_ref, final_g_ref, o_ref, kmean_ref,
                      bias_ref, qa_ref, s_ref, mx_ref, m_ref, acc_ref, og_ref, *, blk, nb, topk, heads, dh):
    i = pl.program_id(1)

    def head_cols(hh):
        return slice(hh * dh, (hh + 1) * dh)

    @pl.when(i == 0)
    def _():
        lane = lax.broadcasted_iota(jnp.int32, (blk, dh), 1)

        def prep_block(j, carry):
            rows = pl.ds(pl.multiple_of(j * blk, blk), blk)
            kmean_ref[pl.ds(j, 1), :] = jnp.mean(k_ref[rows, :].astype(F32), axis=0, keepdims=True)
            bias_ref[j] = jnp.where(lane == j, NEG, 0.0).astype(BF16)
            return carry

        lax.fori_loop(0, nb, prep_block, 0)

    def scores(hh, j):
        kj = k_ref[pl.ds(pl.multiple_of(j * blk, blk), blk), head_cols(hh)]
        k_aug = jnp.concatenate([kj, bias_ref[j]], axis=1)
        return lax.dot_general(k_aug, qa_ref[hh], _TRANS_B, preferred_element_type=F32)

    n_iota = lax.broadcasted_iota(jnp.int32, (nb, blk), 0)
    past = n_iota < i
    k_idx = lax.broadcasted_iota(jnp.int32, (blk, blk), 0)
    q_idx = lax.broadcasted_iota(jnp.int32, (blk, blk), 1)
    causal = k_idx <= q_idx

    def stage1(hh, slot, s):
        s_ref[slot, hh] = s
        mx_ref[slot, hh] = jnp.max(s, axis=0, keepdims=True)

    def stage2(hh, slot, j):
        m_run = m_ref[hh]
        m_new = jnp.maximum(m_run, mx_ref[slot, hh])
        alpha = jnp.exp2(m_run - m_new)
        p = jnp.exp2(s_ref[slot, hh] - m_new).astype(BF16)
        m_ref[hh] = m_new
        acc_ref[hh] = alpha * acc_ref[hh] + jnp.dot(vt_ref[j, hh], p, preferred_element_type=F32)

    gate_scores = []
    for hh in range(heads):
        q = q_ref[:, head_cols(hh)]
        km = kmean_ref[:, head_cols(hh)]
        km_hi = km.astype(BF16)
        km_lo = (km - km_hi.astype(F32)).astype(BF16)
        sc = (lax.dot_general(km_hi, q, _TRANS_B, preferred_element_type=F32)
              + lax.dot_general(km_lo, q, _TRANS_B, preferred_element_type=F32))
        gate_scores.append(jnp.where(past, sc, NEG))

    for hh in range(heads):
        k_own = k_ref[pl.ds(pl.multiple_of(i * blk, blk), blk), head_cols(hh)]
        s_own = lax.dot_general(k_own, q_ref[:, head_cols(hh)], _TRANS_B, preferred_element_type=F32)
        stage1(hh, 0, jnp.where(causal, s_own, NEG))
        m_ref[hh] = jnp.full((1, blk), NEG, F32)
        acc_ref[hh] = jnp.zeros((dh + _SUM_ROWS, blk), F32)

    for hh in range(heads):
        work = gate_scores[hh]
        picked = jnp.zeros((nb, blk), jnp.bool_)
        for _ in range(topk):
            best = jnp.max(work, axis=0, keepdims=True)
            first = jnp.min(jnp.where(work == best, n_iota, nb), axis=0, keepdims=True)
            pick = n_iota == first
            picked = picked | pick
            work = jnp.where(pick, -jnp.inf, work)
        unselected = jnp.where(past & ~picked, 1.0, 0.0)
        unselected = jnp.concatenate([unselected, jnp.zeros((dh - nb, blk), F32)], axis=0)
        qa_ref[hh, :, 0:dh] = q_ref[:, head_cols(hh)]
        qa_ref[hh, :, dh:2 * dh] = unselected.T.astype(BF16)

    def visit(t, cur):
        j_cur = jnp.where(t == 0, i, t - 1)
        stage1(0, 1 - cur, scores(0, t))
        for hh in range(heads):
            if hh + 1 < heads:
                stage1(hh + 1, 1 - cur, scores(hh + 1, t))
            stage2(hh, cur, j_cur)

    def visit_pair(u, carry):
        visit(2 * u, 0)
        visit(2 * u + 1, 1)
        return carry

    lax.fori_loop(0, i // 2, visit_pair, 0)

    j_last = jnp.where(i == 0, i, i - 1)

    @pl.when(i % 2 == 1)
    def _():
        visit(i - 1, 0)
        for hh in range(heads):
            stage2(hh, 1, j_last)

    @pl.when(i % 2 == 0)
    def _():
        for hh in range(heads):
            stage2(hh, 0, j_last)

    y = h_ref[...]
    for hh in range(heads):
        o = (acc_ref[hh, 0:dh, :] / acc_ref[hh, dh:dh + 1, :]).T
        og_ref[:, head_cols(hh)] = (o * _silu(gate_ref[:, head_cols(hh)])).astype(og_ref.dtype)
        if hh % _PROJ_HEADS == _PROJ_HEADS - 1:
            group = slice((hh + 1 - _PROJ_HEADS) * dh, (hh + 1) * dh)
            y = y + jnp.dot(og_ref[:, group], w_out_ref[group, :], preferred_element_type=F32)
    if final_g_ref is not None:
        y = _rmsnorm(y, final_g_ref[...])
    o_ref[...] = y


def _moba_attn_kernel_plain(q_ref, k_ref, vt_ref, gate_ref, h_ref, w_out_ref, o_ref, *scratch, **static):
    _moba_attn_kernel(q_ref, k_ref, vt_ref, gate_ref, h_ref, w_out_ref, None, o_ref, *scratch, **static)


def _moba_attention(q, k, vt, gate, h, w_out, final_g=None, *, batch, seq):
    n, width = q.shape
    d = h.shape[1]
    heads = ATT_HEADS
    assert heads % _PROJ_HEADS == 0
    dh = width // heads
    blk = MOBA_BLOCK
    assert seq % blk == 0
    nb = seq // blk
    assert nb <= dh
    qspec = pl.BlockSpec((blk, width), lambda b, i: (b * nb + i, 0))
    hspec = pl.BlockSpec((blk, d), lambda b, i: (b * nb + i, 0))
    kspec = pl.BlockSpec((seq, width), lambda b, i: (b, 0))
    vtspec = pl.BlockSpec((nb,) + vt.shape[1:], lambda b, i: (b, 0, 0, 0))
    static = dict(blk=blk, nb=nb, topk=MOBA_TOPK, heads=heads, dh=dh)
    in_specs = [qspec, kspec, vtspec, qspec, hspec, _const_spec(w_out.shape)]
    args = [q, k, vt, gate, h, w_out]
    if final_g is None:
        body = functools.partial(_moba_attn_kernel_plain, **static)
    else:
        body = functools.partial(_moba_attn_kernel, **static)
        in_specs.append(_const_spec((1, d)))
        args.append(final_g)
    return pl.pallas_call(
        body,
        grid=(batch, nb),
        in_specs=in_specs,
        out_specs=hspec,
        out_shape=jax.ShapeDtypeStruct((n, d), F32),
        scratch_shapes=[
            pltpu.VMEM((nb, width), F32),
            pltpu.VMEM((nb, blk, dh), BF16),
            pltpu.VMEM((heads, blk, 2 * dh), BF16),
            pltpu.VMEM((2, heads, blk, blk), F32),
            pltpu.VMEM((2, heads, 1, blk), F32),
            pltpu.VMEM((heads, 1, blk), F32),
            pltpu.VMEM((heads, dh + _SUM_ROWS, blk), F32),
            pltpu.VMEM((blk, width), BF16),
        ],
        compiler_params=_params(2),
        name="moba_attention",
    )(*args)


_SUBLANES = 8
_LANES = 128
_SEG_PAD = 4
_CONV_BLOCK = 8


def _conformer_kernel(h_ref, g_ref, w_in_ref, cw_ref, cb_ref, lng_ref, lnb_ref, w_out_ref, o_ref,
                      y_ref, hist_ref, conv_ref, *, tiles_per_seq, kw):
    tm, c = o_ref.shape
    seg = tm // _SUBLANES
    pitch = seg + _SEG_PAD
    halo = kw - 1
    slabs = c // _LANES

    @pl.when(pl.program_id(0) % tiles_per_seq == 0)
    def _():
        hist_ref[...] = jnp.zeros_like(hist_ref)

    h = h_ref[...]
    xn = _rmsnorm(h, g_ref[...]).astype(BF16)
    a = jnp.dot(xn, w_in_ref[:, 0:c], preferred_element_type=F32)
    b = jnp.dot(xn, w_in_ref[:, c:2 * c], preferred_element_type=F32)
    y = a * _sigmoid(b)
    for lc in range(slabs):
        for s in range(_SUBLANES):
            y_ref[lc, s * pitch:s * pitch + seg, :] = y[s * seg:(s + 1) * seg, lc * _LANES:(lc + 1) * _LANES]

    first_segment = lax.broadcasted_iota(jnp.int32, (_SUBLANES, _LANES), 0) == 0
    conv_cols = []
    for lc in range(slabs):
        cols = slice(lc * _LANES, (lc + 1) * _LANES)
        rows = {u: y_ref[lc, pl.ds(u, _SUBLANES, stride=pitch), :] for u in range(seg)}
        for k in range(1, halo + 1):
            before = pltpu.roll(rows[seg - k], 1, 0)
            rows[-k] = jnp.where(first_segment, hist_ref[lc, seg - k:seg - k + 1, :], before)
        for u0 in range(0, seg, _CONV_BLOCK):
            steps = range(u0, min(u0 + _CONV_BLOCK, seg))
            accs = {u: jnp.broadcast_to(cb_ref[:, cols], (_SUBLANES, _LANES)) for u in steps}
            for j0 in range(0, kw, _CONV_BLOCK):
                for j in range(j0, min(j0 + _CONV_BLOCK, kw)):
                    tap = cw_ref[j:j + 1, cols]
                    for u in steps:
                        accs[u] = accs[u] + tap * rows[u - halo + j]
            for u in steps:
                conv_ref[lc, pl.ds(u, _SUBLANES, stride=pitch), :] = accs[u]
        hist_ref[lc] = y[tm - seg:tm, cols]
        conv_cols.append(jnp.concatenate(
            [conv_ref[lc, s * pitch:s * pitch + seg, :] for s in range(_SUBLANES)], axis=0))

    acc = jnp.concatenate(conv_cols, axis=1)
    mu = jnp.mean(acc, axis=-1, keepdims=True)
    cen = acc - mu
    var = jnp.mean(cen * cen, axis=-1, keepdims=True)
    y = cen * lax.rsqrt(var + EPS) * lng_ref[...] + lnb_ref[...]
    gate = jnp.dot(xn, w_in_ref[:, 2 * c:3 * c], preferred_element_type=F32)
    z = (_silu(y) * _silu(gate)).astype(BF16)
    o_ref[...] = h + jnp.dot(z, w_out_ref[...], preferred_element_type=F32)


def _conformer_layer(h, g, w_in, conv_w, conv_b, ln_g, ln_b, w_out, *, seq, tm=1024):
    n, d = h.shape
    c = w_out.shape[0]
    kw = conv_w.shape[0]
    seg = tm // _SUBLANES
    assert seq % tm == 0 and tm % (_SUBLANES * _SUBLANES) == 0 and c % _LANES == 0 and kw - 1 <= seg
    seg_rows = _SUBLANES * (seg + _SEG_PAD)
    return pl.pallas_call(
        functools.partial(_conformer_kernel, tiles_per_seq=seq // tm, kw=kw),
        grid=(n // tm,),
        in_specs=[
            pl.BlockSpec((tm, d), lambda i: (i, 0)),
            _const_spec((1, d)),
            _const_spec(w_in.shape),
            _const_spec(conv_w.shape),
            _const_spec((1, c)),
            _const_spec((1, c)),
            _const_spec((1, c)),
            _const_spec(w_out.shape),
        ],
        out_specs=pl.BlockSpec((tm, d), lambda i: (i, 0)),
        out_shape=jax.ShapeDtypeStruct((n, d), F32),
        scratch_shapes=[
            pltpu.VMEM((c // _LANES, seg_rows, _LANES), F32),
            pltpu.VMEM((c // _LANES, seg, _LANES), F32),
            pltpu.VMEM((c // _LANES, seg_rows, _LANES), F32),
        ],
        compiler_params=_params(1),
        name="conformer_layer",
    )(h, g, w_in, conv_w, conv_b, ln_g, ln_b, w_out)


def _rglru_kernel(h_ref, g_ref, w_in_ref, cw_ref, cb_ref, w_gates_ref, b_rg_ref, b_ig_ref, lam_ref,
                  w_out_ref, o_ref, hseg_ref, oseg_ref, hist_ref, state_ref, *, tiles_per_seq, kw, heads):
    tm, d = o_ref.shape
    w = w_out_ref.shape[0]
    hd = w // heads
    seg = tm // _SUBLANES
    pitch = seg + _SEG_PAD
    halo = kw - 1
    slabs = d // _LANES

    @pl.when(pl.program_id(0) % tiles_per_seq == 0)
    def _():
        hist_ref[...] = jnp.zeros_like(hist_ref)
        state_ref[...] = jnp.zeros_like(state_ref)

    def step(x, u):
        return x[u * _SUBLANES:(u + 1) * _SUBLANES, :]

    for lc in range(slabs):
        for s in range(_SUBLANES):
            hseg_ref[lc, s * pitch:s * pitch + seg, :] = h_ref[s * seg:(s + 1) * seg, lc * _LANES:(lc + 1) * _LANES]
    h = jnp.concatenate(
        [jnp.concatenate([hseg_ref[lc, pl.ds(u, _SUBLANES, stride=pitch), :] for u in range(seg)], axis=0)
         for lc in range(slabs)], axis=1)

    xn = _rmsnorm(h, g_ref[...]).astype(BF16)
    xb = jnp.dot(xn, w_in_ref[:, 0:w], preferred_element_type=F32)

    first_segment = lax.broadcasted_iota(jnp.int32, (_SUBLANES, w), 0) == 0
    rows = {u: step(xb, u) for u in range(seg)}
    for k in range(1, halo + 1):
        rows[-k] = jnp.where(first_segment, hist_ref[k - 1:k, :], pltpu.roll(rows[seg - k], 1, 0))
    for k in range(1, halo + 1):
        hist_ref[k - 1:k, :] = rows[seg - k][_SUBLANES - 1:_SUBLANES, :]
    conv_steps = []
    for u in range(seg):
        acc = cb_ref[...] + cw_ref[0:1, :] * rows[u - halo]
        for j in range(1, kw):
            acc = acc + cw_ref[j:j + 1, :] * rows[u - halo + j]
        conv_steps.append(acc)
    xc = jnp.concatenate(conv_steps, axis=0)

    neg_lam = -lam_ref[...]
    c_softplus = LRU_C * (jnp.maximum(neg_lam, 0.0) + jnp.log1p(jnp.exp(-jnp.abs(neg_lam))))
    neg_c_log2e = -(c_softplus * LOG2_E)
    xc_bf = xc.astype(BF16)
    hs_cols = []
    gate_cols = []
    for hh in range(heads):
        cols = slice(hh * hd, (hh + 1) * hd)
        gates = jnp.dot(xc_bf[:, cols], w_gates_ref[hh], preferred_element_type=F32)
        r = _sigmoid(gates[:, 0:hd] + b_rg_ref[:, cols])
        ig = _sigmoid(gates[:, hd:2 * hd] + b_ig_ref[:, cols])
        a = jnp.exp2(r * neg_c_log2e[:, cols])
        one_minus_a2 = jnp.tanh(r * c_softplus[:, cols]) * (a * a + 1.0)
        b = jnp.sqrt(one_minus_a2) * (ig * xc[:, cols])

        prod = jnp.ones((_SUBLANES, hd), F32)
        total = jnp.zeros((_SUBLANES, hd), F32)
        for u in range(seg):
            au = step(a, u)
            total = au * total + step(b, u)
            prod = au * prod
        cur = state_ref[:, cols]
        starts = []
        for s in range(_SUBLANES):
            starts.append(cur)
            cur = prod[s:s + 1, :] * cur + total[s:s + 1, :]
        state_ref[:, cols] = cur
        hv = jnp.concatenate(starts, axis=0)
        states = []
        for u in range(seg):
            hv = step(a, u) * hv + step(b, u)
            states.append(hv)
        hs_cols.append(jnp.concatenate(states, axis=0))
        if hh % 2 == 1:
            lo = (hh - 1) * hd
            gate_cols.append(jnp.dot(xn, w_in_ref[:, w + lo:w + lo + 2 * hd], preferred_element_type=F32))

    y = (jnp.concatenate(hs_cols, axis=1) * _silu(jnp.concatenate(gate_cols, axis=1))).astype(BF16)
    out = h + jnp.dot(y, w_out_ref[...], preferred_element_type=F32)

    for lc in range(slabs):
        cols = slice(lc * _LANES, (lc + 1) * _LANES)
        for u in range(seg):
            oseg_ref[lc, pl.ds(u, _SUBLANES, stride=pitch), :] = step(out[:, cols], u)
        for s in range(_SUBLANES):
            o_ref[s * seg:(s + 1) * seg, cols] = oseg_ref[lc, s * pitch:s * pitch + seg, :]


def _rglru_layer(h, g, w_in, conv_w, conv_b, w_gates, b_rg, b_ig, lam, w_out, *, seq, tm=1024):
    n, d = h.shape
    w = w_out.shape[0]
    kw = conv_w.shape[0]
    seg = tm // _SUBLANES
    assert seq % tm == 0 and tm % (_SUBLANES * _SUBLANES) == 0 and d % _LANES == 0 and kw - 1 <= _SUBLANES
    assert LRU_HEADS % 2 == 0
    seg_rows = _SUBLANES * (seg + _SEG_PAD)
    return pl.pallas_call(
        functools.partial(_rglru_kernel, tiles_per_seq=seq // tm, kw=kw, heads=LRU_HEADS),
        grid=(n // tm,),
        in_specs=[
            pl.BlockSpec((tm, d), lambda i: (i, 0)),
            _const_spec((1, d)),
            _const_spec(w_in.shape),
            _const_spec(conv_w.shape),
            _const_spec((1, w)),
            _const_spec(w_gates.shape),
            _const_spec((1, w)),
            _const_spec((1, w)),
            _const_spec((1, w)),
            _const_spec(w_out.shape),
        ],
        out_specs=pl.BlockSpec((tm, d), lambda i: (i, 0)),
        out_shape=jax.ShapeDtypeStruct((n, d), F32),
        scratch_shapes=[
            pltpu.VMEM((d // _LANES, seg_rows, _LANES), F32),
            pltpu.VMEM((d // _LANES, seg_rows, _LANES), F32),
            pltpu.VMEM((_SUBLANES, w), F32),
            pltpu.VMEM((1, w), F32),
        ],
        compiler_params=_params(1),
        name="rglru_layer",
    )(h, g, w_in, conv_w, conv_b, w_gates, b_rg, b_ig, lam, w_out)


def _moba_layer(h, norm_g, w_in, w_out, *, batch, seq, final_g=None):
    head_dim = w_out.shape[0] // ATT_HEADS
    q, k, vt, gate = _rms_qkvg(h, norm_g, w_in.astype(BF16), head_dim=head_dim)
    return _moba_attention(q, k, vt, gate, h, w_out.astype(BF16), final_g, batch=batch, seq=seq)


def _row(p):
    return p.reshape(1, -1)


def kernel(x, l0_norm_g, l0_w_in, l0_w_out, l1_norm_g, l1_w_in, l1_conv_w, l1_conv_b, l1_ln_g, l1_ln_b, l1_w_out, l2_norm_g, l2_w_in, l2_conv_w, l2_conv_b, l2_w_rg, l2_b_rg, l2_w_ig, l2_b_ig, l2_lam, l2_w_out, l3_norm_g, l3_w_in, l3_w_out, final_g):
    batch, seq, d = x.shape
    h = x.reshape(batch * seq, d)

    h = _moba_layer(h, _row(l0_norm_g), l0_w_in, l0_w_out, batch=batch, seq=seq)

    h = _conformer_layer(h, _row(l1_norm_g), l1_w_in.astype(BF16), l1_conv_w, _row(l1_conv_b),
                         _row(l1_ln_g), _row(l1_ln_b), l1_w_out.astype(BF16), seq=seq)

    w_gates = jnp.concatenate([l2_w_rg, l2_w_ig], axis=-1).astype(BF16)
    h = _rglru_layer(h, _row(l2_norm_g), l2_w_in.astype(BF16), l2_conv_w, _row(l2_conv_b), w_gates,
                     _row(l2_b_rg), _row(l2_b_ig), _row(l2_lam), l2_w_out.astype(BF16), seq=seq)

    out = _moba_layer(h, _row(l3_norm_g), l3_w_in, l3_w_out, batch=batch, seq=seq, final_g=_row(final_g))
    return out.reshape(batch, seq, d)
```

```python
import functools

import jax
import jax.numpy as jnp
from jax import lax
from jax.experimental import pallas as pl
from jax.experimental.pallas import tpu as pltpu

F32 = jnp.float32
BF16 = jnp.bfloat16

EPS = 1e-6
NEG = -1e30

ATT_HEADS = 8
MOBA_BLOCK = 256
MOBA_TOPK = 3
LRU_HEADS = 10
LRU_C = 8.0

VMEM_LIMIT_BYTES = 56 * 1024 * 1024
LOG2_E = 1.4426950408889634


def _const_spec(shape):
    zeros = (0,) * len(shape)
    return pl.BlockSpec(shape, lambda *_: zeros, pipeline_mode=pl.Buffered(1))


def _params(n_axes):
    return pltpu.CompilerParams(
        dimension_semantics=("arbitrary",) * n_axes,
        vmem_limit_bytes=VMEM_LIMIT_BYTES,
    )


def _rmsnorm(x, g):
    return x * lax.rsqrt(jnp.mean(x * x, axis=-1, keepdims=True) + EPS) * g


def _sigmoid(x):
    return 0.5 * jnp.tanh(0.5 * x) + 0.5


def _silu(x):
    u = 0.5 * x
    return u * jnp.tanh(u) + u


_SUM_ROWS = 16


def _rms_qkvg_kernel(x_ref, g_ref, w_ref, q_ref, k_ref, vt_ref, gate_ref, *, q_scale):
    xn = _rmsnorm(x_ref[...], g_ref[...]).astype(BF16)
    width = q_ref.shape[1]
    blocks, heads, rows, blk = vt_ref.shape
    dh = rows - _SUM_ROWS

    def project(c):
        return jnp.dot(xn, w_ref[:, c * width:(c + 1) * width], preferred_element_type=F32)

    v = project(2)
    for jb in range(blocks):
        for hh in range(heads):
            vt_ref[jb, hh, 0:dh, :] = v[jb * blk:(jb + 1) * blk, hh * dh:(hh + 1) * dh].T.astype(BF16)
            vt_ref[jb, hh, dh:rows, :] = jnp.ones((_SUM_ROWS, blk), BF16)
    q_ref[...] = (project(0) * q_scale).astype(q_ref.dtype)
    k_ref[...] = project(1).astype(k_ref.dtype)
    gate_ref[...] = _silu(project(3))


def _rms_qkvg(h, g, w_in, *, head_dim, tm=1024):
    n, d = h.shape
    width = w_in.shape[1] // 4
    heads = width // head_dim
    blk = MOBA_BLOCK
    assert tm % blk == 0
    out_bf = jax.ShapeDtypeStruct((n, width), BF16)
    row_spec = pl.BlockSpec((tm, width), lambda i: (i, 0))
    vt_shape = (n // blk, heads, head_dim + _SUM_ROWS, blk)
    vt_spec = pl.BlockSpec((tm // blk,) + vt_shape[1:], lambda i: (i, 0, 0, 0))
    return pl.pallas_call(
        functools.partial(_rms_qkvg_kernel, q_scale=head_dim ** -0.5 * LOG2_E),
        grid=(n // tm,),
        in_specs=[
            pl.BlockSpec((tm, d), lambda i: (i, 0)),
            _const_spec((1, d)),
            _const_spec(w_in.shape),
        ],
        out_specs=[row_spec, row_spec, vt_spec, row_spec],
        out_shape=[out_bf, out_bf, jax.ShapeDtypeStruct(vt_shape, BF16), jax.ShapeDtypeStruct((n, width), F32)],
        compiler_params=_params(1),
        name="rms_qkvg",
    )(h, g, w_in)


_TRANS_B = (((1,), (1,)), ((), ()))
_TRANS_A = (((0,), (0,)), ((), ()))


def _moba_attn_kernel(q_ref, k_ref, vt_ref, gate_ref, h_ref, w_out_ref, final_g_ref, o_ref, kmean_ref,
                      bias_ref, qa_ref, s_ref, mx_ref, m_ref, acc_ref, og_ref, *, blk, nb, topk, heads, dh):
    i = pl.program_id(1)

    def head_cols(hh):
        return slice(hh * dh, (hh + 1) * dh)

    @pl.when(i == 0)
    def _():
        lane = lax.broadcasted_iota(jnp.int32, (blk, dh), 1)

        def prep_block(j, carry):
            rows = pl.ds(pl.multiple_of(j * blk, blk), blk)
            kmean_ref[pl.ds(j, 1), :] = jnp.mean(k_ref[rows, :].astype(F32), axis=0, keepdims=True)
            bias_ref[j] = jnp.where(lane == j, NEG, 0.0).astype(BF16)
            return carry

        lax.fori_loop(0, nb, prep_block, 0)

    def scores(hh, j):
        kj = k_ref[pl.ds(pl.multiple_of(j * blk, blk), blk), head_cols(hh)]
        k_aug = jnp.concatenate([kj, bias_ref[j]], axis=1)
        return lax.dot_general(k_aug, qa_ref[hh], _TRANS_B, preferred_element_type=F32)

    n_iota = lax.broadcasted_iota(jnp.int32, (nb, blk), 0)
    past = n_iota < i
    k_idx = lax.broadcasted_iota(jnp.int32, (blk, blk), 0)
    q_idx = lax.broadcasted_iota(jnp.int32, (blk, blk), 1)
    causal = k_idx <= q_idx

    def stage1(hh, slot, s):
        s_ref[slot, hh] = s
        mx_ref[slot, hh] = jnp.max(s, axis=0, keepdims=True)

    def stage2(hh, slot, j):
        m_run = m_ref[hh]
        m_new = jnp.maximum(m_run, mx_ref[slot, hh])
        alpha = jnp.exp2(m_run - m_new)
        p = jnp.exp2(s_ref[slot, hh] - m_new).astype(BF16)
        m_ref[hh] = m_new
        acc_ref[hh] = alpha * acc_ref[hh] + jnp.dot(vt_ref[j, hh], p, preferred_element_type=F32)

    gate_scores = []
    for hh in range(heads):
        q = q_ref[:, head_cols(hh)]
        km = kmean_ref[:, head_cols(hh)]
        km_hi = km.astype(BF16)
        km_lo = (km - km_hi.astype(F32)).astype(BF16)
        sc = (lax.dot_general(km_hi, q, _TRANS_B, preferred_element_type=F32)
              + lax.dot_general(km_lo, q, _TRANS_B, preferred_element_type=F32))
        gate_scores.append(jnp.where(past, sc, NEG))

    for hh in range(heads):
        k_own = k_ref[pl.ds(pl.multiple_of(i * blk, blk), blk), head_cols(hh)]
        s_own = lax.dot_general(k_own, q_ref[:, head_cols(hh)], _TRANS_B, preferred_element_type=F32)
        stage1(hh, 0, jnp.where(causal, s_own, NEG))
        m_ref[hh] = jnp.full((1, blk), NEG, F32)
        acc_ref[hh] = jnp.zeros((dh + _SUM_ROWS, blk), F32)

    for hh in range(heads):
        work = gate_scores[hh]
        picked = jnp.zeros((nb, blk), jnp.bool_)
        for _ in range(topk):
            best = jnp.max(work, axis=0, keepdims=True)
            first = jnp.min(jnp.where(work == best, n_iota, nb), axis=0, keepdims=True)
            pick = n_iota == first
            picked = picked | pick
            work = jnp.where(pick, -jnp.inf, work)
        unselected = jnp.where(past & ~picked, 1.0, 0.0)
        unselected = jnp.concatenate([unselected, jnp.zeros((dh - nb, blk), F32)], axis=0)
        qa_ref[hh, :, 0:dh] = q_ref[:, head_cols(hh)]
        qa_ref[hh, :, dh:2 * dh] = unselected.T.astype(BF16)

    def visit(t, cur):
        j_cur = jnp.where(t == 0, i, t - 1)
        for hh in range(heads):
            stage1(hh, 1 - cur, scores(hh, t))
            stage2(hh, cur, j_cur)

    def visit_pair(u, carry):
        visit(2 * u, 0)
        visit(2 * u + 1, 1)
        return carry

    lax.fori_loop(0, i // 2, visit_pair, 0)

    j_last = jnp.where(i == 0, i, i - 1)

    @pl.when(i % 2 == 1)
    def _():
        visit(i - 1, 0)
        for hh in range(heads):
            stage2(hh, 1, j_last)

    @pl.when(i % 2 == 0)
    def _():
        for hh in range(heads):
            stage2(hh, 0, j_last)

    y = h_ref[...]
    for hh in range(heads):
        o = (acc_ref[hh, 0:dh, :] / acc_ref[hh, dh:dh + 1, :]).T
        og_ref[:, head_cols(hh)] = (o * gate_ref[:, head_cols(hh)]).astype(og_ref.dtype)
        if hh % 2 == 1:
            pair = slice((hh - 1) * dh, (hh + 1) * dh)
            y = y + jnp.dot(og_ref[:, pair], w_out_ref[pair, :], preferred_element_type=F32)
    if final_g_ref is not None:
        y = _rmsnorm(y, final_g_ref[...])
    o_ref[...] = y


def _moba_attn_kernel_plain(q_ref, k_ref, vt_ref, gate_ref, h_ref, w_out_ref, o_ref, *scratch, **static):
    _moba_attn_kernel(q_ref, k_ref, vt_ref, gate_ref, h_ref, w_out_ref, None, o_ref, *scratch, **static)


def _moba_attention(q, k, vt, gate, h, w_out, final_g=None, *, batch, seq):
    n, width = q.shape
    d = h.shape[1]
    heads = ATT_HEADS
    assert heads % 2 == 0
    dh = width // heads
    blk = MOBA_BLOCK
    assert seq % blk == 0
    nb = seq // blk
    assert nb <= dh
    qspec = pl.BlockSpec((blk, width), lambda b, i: (b * nb + i, 0))
    hspec = pl.BlockSpec((blk, d), lambda b, i: (b * nb + i, 0))
    kspec = pl.BlockSpec((seq, width), lambda b, i: (b, 0))
    vtspec = pl.BlockSpec((nb,) + vt.shape[1:], lambda b, i: (b, 0, 0, 0))
    static = dict(blk=blk, nb=nb, topk=MOBA_TOPK, heads=heads, dh=dh)
    in_specs = [qspec, kspec, vtspec, qspec, hspec, _const_spec(w_out.shape)]
    args = [q, k, vt, gate, h, w_out]
    if final_g is None:
        body = functools.partial(_moba_attn_kernel_plain, **static)
    else:
        body = functools.partial(_moba_attn_kernel, **static)
        in_specs.append(_const_spec((1, d)))
        args.append(final_g)
    return pl.pallas_call(
        body,
        grid=(batch, nb),
        in_specs=in_specs,
        out_specs=hspec,
        out_shape=jax.ShapeDtypeStruct((n, d), F32),
        scratch_shapes=[
            pltpu.VMEM((nb, width), F32),
            pltpu.VMEM((nb, blk, dh), BF16),
            pltpu.VMEM((heads, blk, 2 * dh), BF16),
            pltpu.VMEM((2, heads, blk, blk), F32),
            pltpu.VMEM((2, heads, 1, blk), F32),
            pltpu.VMEM((heads, 1, blk), F32),
            pltpu.VMEM((heads, dh + _SUM_ROWS, blk), F32),
            pltpu.VMEM((blk, width), BF16),
        ],
        compiler_params=_params(2),
        name="moba_attention",
    )(*args)


_SUBLANES = 8
_LANES = 128
_SEG_PAD = 4
_CONV_BLOCK = 8


def _conformer_kernel(h_ref, g_ref, w_in_ref, cw_ref, cb_ref, lng_ref, lnb_ref, w_out_ref, o_ref,
                      y_ref, hist_ref, conv_ref, *, tiles_per_seq, kw):
    tm, c = o_ref.shape
    seg = tm // _SUBLANES
    pitch = seg + _SEG_PAD
    halo = kw - 1
    slabs = c // _LANES

    @pl.when(pl.program_id(0) % tiles_per_seq == 0)
    def _():
        hist_ref[...] = jnp.zeros_like(hist_ref)

    h = h_ref[...]
    xn = _rmsnorm(h, g_ref[...]).astype(BF16)
    a = jnp.dot(xn, w_in_ref[:, 0:c], preferred_element_type=F32)
    b = jnp.dot(xn, w_in_ref[:, c:2 * c], preferred_element_type=F32)
    y = a * _sigmoid(b)
    for lc in range(slabs):
        for s in range(_SUBLANES):
            y_ref[lc, s * pitch:s * pitch + seg, :] = y[s * seg:(s + 1) * seg, lc * _LANES:(lc + 1) * _LANES]

    first_segment = lax.broadcasted_iota(jnp.int32, (_SUBLANES, _LANES), 0) == 0
    conv_cols = []
    for lc in range(slabs):
        cols = slice(lc * _LANES, (lc + 1) * _LANES)
        rows = {u: y_ref[lc, pl.ds(u, _SUBLANES, stride=pitch), :] for u in range(seg)}
        for k in range(1, halo + 1):
            before = pltpu.roll(rows[seg - k], 1, 0)
            rows[-k] = jnp.where(first_segment, hist_ref[lc, seg - k:seg - k + 1, :], before)
        for u0 in range(0, seg, _CONV_BLOCK):
            steps = range(u0, min(u0 + _CONV_BLOCK, seg))
            accs = {u: jnp.broadcast_to(cb_ref[:, cols], (_SUBLANES, _LANES)) for u in steps}
            for j0 in range(0, kw, _CONV_BLOCK):
                for j in range(j0, min(j0 + _CONV_BLOCK, kw)):
                    tap = cw_ref[j:j + 1, cols]
                    for u in steps:
                        accs[u] = accs[u] + tap * rows[u - halo + j]
            for u in steps:
                conv_ref[lc, pl.ds(u, _SUBLANES, stride=pitch), :] = accs[u]
        hist_ref[lc] = y[tm - seg:tm, cols]
        conv_cols.append(jnp.concatenate(
            [conv_ref[lc, s * pitch:s * pitch + seg, :] for s in range(_SUBLANES)], axis=0))

    acc = jnp.concatenate(conv_cols, axis=1)
    mu = jnp.mean(acc, axis=-1, keepdims=True)
    cen = acc - mu
    var = jnp.mean(cen * cen, axis=-1, keepdims=True)
    y = cen * lax.rsqrt(var + EPS) * lng_ref[...] + lnb_ref[...]
    gate = jnp.dot(xn, w_in_ref[:, 2 * c:3 * c], preferred_element_type=F32)
    z = (_silu(y) * _silu(gate)).astype(BF16)
    o_ref[...] = h + jnp.dot(z, w_out_ref[...], preferred_element_type=F32)


def _conformer_layer(h, g, w_in, conv_w, conv_b, ln_g, ln_b, w_out, *, seq, tm=1024):
    n, d = h.shape
    c = w_out.shape[0]
    kw = conv_w.shape[0]
    seg = tm // _SUBLANES
    assert seq % tm == 0 and tm % (_SUBLANES * _SUBLANES) == 0 and c % _LANES == 0 and kw - 1 <= seg
    seg_rows = _SUBLANES * (seg + _SEG_PAD)
    return pl.pallas_call(
        functools.partial(_conformer_kernel, tiles_per_seq=seq // tm, kw=kw),
        grid=(n // tm,),
        in_specs=[
            pl.BlockSpec((tm, d), lambda i: (i, 0)),
            _const_spec((1, d)),
            _const_spec(w_in.shape),
            _const_spec(conv_w.shape),
            _const_spec((1, c)),
            _const_spec((1, c)),
            _const_spec((1, c)),
            _const_spec(w_out.shape),
        ],
        out_specs=pl.BlockSpec((tm, d), lambda i: (i, 0)),
        out_shape=jax.ShapeDtypeStruct((n, d), F32),
        scratch_shapes=[
            pltpu.VMEM((c // _LANES, seg_rows, _LANES), F32),
            pltpu.VMEM((c // _LANES, seg, _LANES), F32),
            pltpu.VMEM((c // _LANES, seg_rows, _LANES), F32),
        ],
        compiler_params=_params(1),
        name="conformer_layer",
    )(h, g, w_in, conv_w, conv_b, ln_g, ln_b, w_out)


def _rglru_kernel(h_ref, g_ref, w_in_ref, cw_ref, cb_ref, w_gates_ref, b_rg_ref, b_ig_ref, lam_ref,
                  w_out_ref, o_ref, hseg_ref, oseg_ref, hist_ref, state_ref, *, tiles_per_seq, kw, heads):
    tm, d = o_ref.shape
    w = w_out_ref.shape[0]
    hd = w // heads
    seg = tm // _SUBLANES
    pitch = seg + _SEG_PAD
    halo = kw - 1
    slabs = d // _LANES

    @pl.when(pl.program_id(0) % tiles_per_seq == 0)
    def _():
        hist_ref[...] = jnp.zeros_like(hist_ref)
        state_ref[...] = jnp.zeros_like(state_ref)

    def step(x, u):
        return x[u * _SUBLANES:(u + 1) * _SUBLANES, :]

    for lc in range(slabs):
        for s in range(_SUBLANES):
            hseg_ref[lc, s * pitch:s * pitch + seg, :] = h_ref[s * seg:(s + 1) * seg, lc * _LANES:(lc + 1) * _LANES]
    h = jnp.concatenate(
        [jnp.concatenate([hseg_ref[lc, pl.ds(u, _SUBLANES, stride=pitch), :] for u in range(seg)], axis=0)
         for lc in range(slabs)], axis=1)

    xn = _rmsnorm(h, g_ref[...]).astype(BF16)
    xb = jnp.dot(xn, w_in_ref[:, 0:w], preferred_element_type=F32)

    first_segment = lax.broadcasted_iota(jnp.int32, (_SUBLANES, w), 0) == 0
    rows = {u: step(xb, u) for u in range(seg)}
    for k in range(1, halo + 1):
        rows[-k] = jnp.where(first_segment, hist_ref[k - 1:k, :], pltpu.roll(rows[seg - k], 1, 0))
    for k in range(1, halo + 1):
        hist_ref[k - 1:k, :] = rows[seg - k][_SUBLANES - 1:_SUBLANES, :]
    conv_steps = []
    for u in range(seg):
        acc = cb_ref[...] + cw_ref[0:1, :] * rows[u - halo]
        for j in range(1, kw):
            acc = acc + cw_ref[j:j + 1, :] * rows[u - halo + j]
        conv_steps.append(acc)
    xc = jnp.concatenate(conv_steps, axis=0)

    neg_lam = -lam_ref[...]
    c_softplus = LRU_C * (jnp.maximum(neg_lam, 0.0) + jnp.log1p(jnp.exp(-jnp.abs(neg_lam))))
    neg_c_log2e = -(c_softplus * LOG2_E)
    xc_bf = xc.astype(BF16)
    hs_cols = []
    gate_cols = []
    for hh in range(heads):
        cols = slice(hh * hd, (hh + 1) * hd)
        gates = jnp.dot(xc_bf[:, cols], w_gates_ref[hh], preferred_element_type=F32)
        r = _sigmoid(gates[:, 0:hd] + b_rg_ref[:, cols])
        ig = _sigmoid(gates[:, hd:2 * hd] + b_ig_ref[:, cols])
        a = jnp.exp2(r * neg_c_log2e[:, cols])
        one_minus_a2 = jnp.tanh(r * c_softplus[:, cols]) * (a * a + 1.0)
        b = jnp.sqrt(one_minus_a2) * (ig * xc[:, cols])

        prod = jnp.ones((_SUBLANES, hd), F32)
        total = jnp.zeros((_SUBLANES, hd), F32)
        for u in range(seg):
            au = step(a, u)
            total = au * total + step(b, u)
            prod = au * prod
        cur = state_ref[:, cols]
        starts = []
        for s in range(_SUBLANES):
            starts.append(cur)
            cur = prod[s:s + 1, :] * cur + total[s:s + 1, :]
        state_ref[:, cols] = cur
        hv = jnp.concatenate(starts, axis=0)
        states = []
        for u in range(seg):
            hv = step(a, u) * hv + step(b, u)
            states.append(hv)
        hs_cols.append(jnp.concatenate(states, axis=0))
        if hh % 2 == 1:
            lo = (hh - 1) * hd
            gate_cols.append(jnp.dot(xn, w_in_ref[:, w + lo:w + lo + 2 * hd], preferred_element_type=F32))

    y = (jnp.concatenate(hs_cols, axis=1) * _silu(jnp.concatenate(gate_cols, axis=1))).astype(BF16)
    out = h + jnp.dot(y, w_out_ref[...], preferred_element_type=F32)

    for lc in range(slabs):
        cols = slice(lc * _LANES, (lc + 1) * _LANES)
        for u in range(seg):
            oseg_ref[lc, pl.ds(u, _SUBLANES, stride=pitch), :] = step(out[:, cols], u)
        for s in range(_SUBLANES):
            o_ref[s * seg:(s + 1) * seg, cols] = oseg_ref[lc, s * pitch:s * pitch + seg, :]


def _rglru_layer(h, g, w_in, conv_w, conv_b, w_gates, b_rg, b_ig, lam, w_out, *, seq, tm=1024):
    n, d = h.shape
    w = w_out.shape[0]
    kw = conv_w.shape[0]
    seg = tm // _SUBLANES
    assert seq % tm == 0 and tm % (_SUBLANES * _SUBLANES) == 0 and d % _LANES == 0 and kw - 1 <= _SUBLANES
    assert LRU_HEADS % 2 == 0
    seg_rows = _SUBLANES * (seg + _SEG_PAD)
    return pl.pallas_call(
        functools.partial(_rglru_kernel, tiles_per_seq=seq // tm, kw=kw, heads=LRU_HEADS),
        grid=(n // tm,),
        in_specs=[
            pl.BlockSpec((tm, d), lambda i: (i, 0)),
            _const_spec((1, d)),
            _const_spec(w_in.shape),
            _const_spec(conv_w.shape),
            _const_spec((1, w)),
            _const_spec(w_gates.shape),
            _const_spec((1, w)),
            _const_spec((1, w)),
            _const_spec((1, w)),
            _const_spec(w_out.shape),
        ],
        out_specs=pl.BlockSpec((tm, d), lambda i: (i, 0)),
        out_shape=jax.ShapeDtypeStruct((n, d), F32),
        scratch_shapes=[
            pltpu.VMEM((d // _LANES, seg_rows, _LANES), F32),
            pltpu.VMEM((d // _LANES, seg_rows, _LANES), F32),
            pltpu.VMEM((_SUBLANES, w), F32),
            pltpu.VMEM((1, w), F32),
        ],
        compiler_params=_params(1),
        name="rglru_layer",
    )(h, g, w_in, conv_w, conv_b, w_gates, b_rg, b_ig, lam, w_out)


def _moba_layer(h, norm_g, w_in, w_out, *, batch, seq, final_g=None):
    head_dim = w_out.shape[0] // ATT_HEADS
    q, k, vt, gate = _rms_qkvg(h, norm_g, w_in.astype(BF16), head_dim=head_dim)
    return _moba_attention(q, k, vt, gate, h, w_out.astype(BF16), final_g, batch=batch, seq=seq)


def _row(p):
    return p.reshape(1, -1)


def kernel(x, l0_norm_g, l0_w_in, l0_w_out, l1_norm_g, l1_w_in, l1_conv_w, l1_conv_b, l1_ln_g, l1_ln_b, l1_w_out, l2_norm_g, l2_w_in, l2_conv_w, l2_conv_b, l2_w_rg, l2_b_rg, l2_w_ig, l2_b_ig, l2_lam, l2_w_out, l3_norm_g, l3_w_in, l3_w_out, final_g):
    batch, seq, d = x.shape
    h = x.reshape(batch * seq, d)

    h = _moba_layer(h, _row(l0_norm_g), l0_w_in, l0_w_out, batch=batch, seq=seq)

    h = _conformer_layer(h, _row(l1_norm_g), l1_w_in.astype(BF16), l1_conv_w, _row(l1_conv_b),
                         _row(l1_ln_g), _row(l1_ln_b), l1_w_out.astype(BF16), seq=seq)

    w_gates = jnp.concatenate([l2_w_rg, l2_w_ig], axis=-1).astype(BF16)
    h = _rglru_layer(h, _row(l2_norm_g), l2_w_in.astype(BF16), l2_conv_w, _row(l2_conv_b), w_gates,
                     _row(l2_b_rg), _row(l2_b_ig), _row(l2_lam), l2_w_out.astype(BF16), seq=seq)

    out = _moba_layer(h, _row(l3_norm_g), l3_w_in, l3_w_out, batch=batch, seq=seq, final_g=_row(final_g))
    return out.reshape(batch, seq, d)
```
